```python
import jax, jax.numpy as jnp
from jax import lax
import numpy as np

D_MODEL = 1024
BATCH = 8
SEQ = 2048
DEPTH = 4

EXPAND = 2
E_INNER = EXPAND * D_MODEL
HEAD_DIM = 128
E_A = E_INNER // 2
E_B = E_INNER - E_A
H_A = E_A // HEAD_DIM
H_B = E_B // HEAD_DIM
CONV_WIDTH = 3
CHUNK = 128
AB_SPLITS = (E_A, E_A, E_A, E_A, E_B, E_B, E_B)
AB_IN = sum(AB_SPLITS)
E_C = E_INNER
POOL_WINDOWS = (2, 4, 8, 16)
N_POOL_GROUPS = len(POOL_WINDOWS)
G_C = E_C // N_POOL_GROUPS
N_EVEN = (DEPTH + 1) // 2
N_ODD = DEPTH // 2
EPS = 1e-6

kernel_name = "hybrid_shortconv_sgu_pool_adaln_trunk"


def rmsnorm(x, g):
    x32 = x.astype(jnp.float32)
    y = x32 * lax.rsqrt(jnp.mean(x32 * x32, axis=-1, keepdims=True) + EPS)
    return (y * g.astype(jnp.float32)).astype(x.dtype)


def modulate(h, shift, scale):
    return h * (1 + scale[:, None, :]) + shift[:, None, :]


def causal_short_conv(x, w):
    S = x.shape[1]
    xp = jnp.pad(x, ((0, 0), (CONV_WIDTH - 1, 0), (0, 0)))
    y = xp[:, 0:S] * w[0]
    for k in range(1, CONV_WIDTH):
        y = y + xp[:, k:k + S] * w[k]
    return y


def chunked_sgu(u, v, ln_g, ln_b, w_s, b_s):
    Bn, S, _ = v.shape
    n_chunks = S // CHUNK
    v32 = v.astype(jnp.float32).reshape(Bn, S, H_B, HEAD_DIM)
    mu = jnp.mean(v32, axis=-1, keepdims=True)
    var = jnp.mean(jnp.square(v32 - mu), axis=-1, keepdims=True)
    vn = ((v32 - mu) * lax.rsqrt(var + EPS)).reshape(Bn, S, E_B)
    vn = (vn * ln_g.astype(jnp.float32) + ln_b.astype(jnp.float32)).astype(v.dtype)
    vn = vn.reshape(Bn, n_chunks, CHUNK, H_B, HEAD_DIM)
    causal = jnp.tril(jnp.ones((CHUNK, CHUNK), dtype=bool))
    w_masked = jnp.where(causal[None], w_s, jnp.zeros_like(w_s))
    mixed = jnp.einsum('hts,bnshd->bnthd', w_masked, vn) + b_s.T[None, None, :, :, None]
    return u * mixed.reshape(Bn, S, E_B)


def multiscale_pool(p):
    S = p.shape[1]
    p32 = p.astype(jnp.float32)
    cs = jnp.cumsum(p32, axis=1)
    outs = []
    for gi, win in enumerate(POOL_WINDOWS):
        sl = slice(gi * G_C, (gi + 1) * G_C)
        csg = cs[..., sl]
        prev = jnp.pad(csg, ((0, 0), (win, 0), (0, 0)))[:, :S]
        cnt = jnp.minimum(jnp.arange(1, S + 1), win).astype(jnp.float32)[None, :, None]
        outs.append((csg - prev) / cnt - p32[..., sl])
    return jnp.stack(outs, axis=2).astype(p.dtype)


def even_mixer(h, w_in, conv_w, ln_g, ln_b, w_s, b_s, w_out):
    proj = h @ w_in
    idx = [int(i) for i in np.cumsum(AB_SPLITS)[:-1]]
    a_h, a_b, a_c, a_z, b_u, b_v, b_z = jnp.split(proj, idx, axis=-1)
    y_a = a_b * causal_short_conv(a_c * a_h, conv_w)
    y_a = y_a * jax.nn.silu(a_z)
    y_b = chunked_sgu(b_u, b_v, ln_g, ln_b, w_s, b_s)
    y_b = y_b * jax.nn.silu(b_z)
    return jnp.concatenate([y_a, y_b], axis=-1) @ w_out


def odd_mixer(h, w_in, pool_w, pool_scale, w_out):
    Bn, S, _ = h.shape
    proj = h @ w_in
    p, z = jnp.split(proj, 2, axis=-1)
    pooled = multiscale_pool(p)
    y = jnp.einsum('bsgi,gio->bsgo', pooled, pool_w).reshape(Bn, S, E_C)
    y = y * pool_scale * jax.nn.silu(z)
    return y @ w_out


def setup_inputs(seed: int = 0) -> dict:
    key = jax.random.key(seed)
    ks = jax.random.split(key, 20)
    nrm = jax.random.normal
    f32 = jnp.float32
    return {
        "x": nrm(ks[0], (BATCH, SEQ, D_MODEL), f32),
        "c": nrm(ks[1], (BATCH, D_MODEL), f32),
        "norm_g": 1.0 + 0.1 * nrm(ks[2], (DEPTH, D_MODEL), f32),
        "ada_w": nrm(ks[3], (DEPTH, D_MODEL, 3 * D_MODEL), f32) * D_MODEL ** -0.5,
        "ada_b": 0.01 * nrm(ks[4], (DEPTH, 3 * D_MODEL), f32),
        "ab_w_in": nrm(ks[5], (N_EVEN, D_MODEL, AB_IN), f32) * D_MODEL ** -0.5,
        "ab_conv_w": nrm(ks[6], (N_EVEN, CONV_WIDTH, E_A), f32) * CONV_WIDTH ** -0.5,
        "ab_ln_g": 1.0 + 0.1 * nrm(ks[7], (N_EVEN, E_B), f32),
        "ab_ln_b": 0.02 * nrm(ks[8], (N_EVEN, E_B), f32),
        "ab_sgu_w": nrm(ks[9], (N_EVEN, H_B, CHUNK, CHUNK), f32) * CHUNK ** -0.5,
        "ab_sgu_b": 1.0 + 0.1 * nrm(ks[10], (N_EVEN, H_B, CHUNK), f32),
        "ab_w_out": nrm(ks[11], (N_EVEN, E_A + E_B, D_MODEL), f32) * (E_A + E_B) ** -0.5,
        "c_w_in": nrm(ks[12], (N_ODD, D_MODEL, 2 * E_C), f32) * D_MODEL ** -0.5,
        "c_pool_w": nrm(ks[13], (N_ODD, N_POOL_GROUPS, G_C, G_C), f32) * G_C ** -0.5,
        "c_pool_scale": 1.0 + 0.1 * nrm(ks[14], (N_ODD, E_C), f32),
        "c_w_out": nrm(ks[15], (N_ODD, E_C, D_MODEL), f32) * E_C ** -0.5,
        "final_g": 1.0 + 0.1 * nrm(ks[16], (D_MODEL,), f32),
    }


def reference(x, c, norm_g, ada_w, ada_b, ab_w_in, ab_conv_w, ab_ln_g, ab_ln_b,
              ab_sgu_w, ab_sgu_b, ab_w_out, c_w_in, c_pool_w, c_pool_scale,
              c_w_out, final_g):
    c_act = jax.nn.silu(c)
    for i in range(DEPTH):
        mod = c_act @ ada_w[i] + ada_b[i]
        shift, scale, gate = jnp.split(mod, 3, axis=-1)
        h = modulate(rmsnorm(x, norm_g[i]), shift, scale)
        j = i // 2
        if i % 2 == 0:
            out = even_mixer(h, ab_w_in[j], ab_conv_w[j], ab_ln_g[j], ab_ln_b[j],
                             ab_sgu_w[j], ab_sgu_b[j], ab_w_out[j])
        else:
            out = odd_mixer(h, c_w_in[j], c_pool_w[j], c_pool_scale[j], c_w_out[j])
        x = x + gate[:, None, :] * out
    return rmsnorm(x, final_g)
```

```python
import functools

import jax
import jax.numpy as jnp
from jax import lax
from jax.experimental import pallas as pl
from jax.experimental.pallas import tpu as pltpu

D_MODEL = 1024
DEPTH = 4
E_INNER = 2 * D_MODEL
HEAD_DIM = 128
E_A = E_INNER // 2
E_B = E_INNER - E_A
H_B = E_B // HEAD_DIM
CONV_WIDTH = 3
CHUNK = 128
AB_IN = 4 * E_A + 3 * E_B
E_C = E_INNER
POOL_WINDOWS = (2, 4, 8, 16)
G_C = E_C // len(POOL_WINDOWS)
EPS = 1e-6

SEQ_TILE = 512
COL_BLOCK = 256
CONV_HALO = 8
POOL_HALO = 16
MOD_COLS = 1024
VMEM_LIMIT_BYTES = 56 * 1024 * 1024

_F32 = jnp.float32
_BF16 = jnp.bfloat16


def _dot(a, b):
    return jnp.dot(a, b, preferred_element_type=_F32)


def _silu(v):
    return v * jax.nn.sigmoid(v)


def _rmsnorm(x, g):
    return x * lax.rsqrt(jnp.mean(x * x, axis=-1, keepdims=True) + EPS) * g


def _resident(block_shape, index_map):
    return pl.BlockSpec(block_shape, index_map, pipeline_mode=pl.Buffered(1))


def _mod_kernel(c_ref, w_ref, b_ref, o_ref):
    c_act = _silu(c_ref[...]).astype(_BF16)
    o_ref[...] = _dot(c_act, w_ref[...].astype(_BF16)) + b_ref[...]


def _modulation(c, ada_w, ada_b):
    batch = c.shape[0]
    n_col = 3 * D_MODEL
    return pl.pallas_call(
        _mod_kernel,
        grid=(DEPTH, n_col // MOD_COLS),
        in_specs=[
            pl.BlockSpec((batch, D_MODEL), lambda i, j: (0, 0)),
            pl.BlockSpec((None, D_MODEL, MOD_COLS), lambda i, j: (i, 0, j)),
            pl.BlockSpec((None, 1, MOD_COLS), lambda i, j: (i, 0, j)),
        ],
        out_specs=pl.BlockSpec((None, batch, MOD_COLS), lambda i, j: (i, 0, j)),
        out_shape=jax.ShapeDtypeStruct((DEPTH, batch, n_col), _F32),
        compiler_params=pltpu.CompilerParams(
            dimension_semantics=("arbitrary", "arbitrary")),
        name="adaln_modulation",
    )(c, ada_w, ada_b.reshape(DEPTH, 1, n_col))


def _modulated_input(x_ref, mod_ref, g_ref, h_scr):
    x = x_ref[...]
    h = _rmsnorm(x, g_ref[...]) * (1.0 + mod_ref[1]) + mod_ref[0]
    h_scr[...] = h.astype(_BF16)


def _residual_out(x_ref, mod_ref, out, final_g_ref, o_ref):
    res = x_ref[...] + mod_ref[2] * out
    if final_g_ref is not None:
        res = _rmsnorm(res, final_g_ref[...])
    o_ref[...] = res


def _even_kernel(x_ref, mod_ref, g_ref, w_in_ref, conv_ref, lng_ref, lnb_ref,
                 sgw_ref, sgb_ref, w_out_ref, *rest, final_norm):
    if final_norm:
        final_g_ref, o_ref, h_scr, y_scr, ch_scr = rest
    else:
        final_g_ref = None
        o_ref, h_scr, y_scr, ch_scr = rest
    ts = x_ref.shape[0]
    b = pl.program_id(0)
    s = pl.program_id(1)

    @pl.when((b == 0) & (s == 0))
    def _():
        ch_scr[ts:ts + CONV_HALO, :] = jnp.zeros((CONV_HALO, E_A), _F32)

    prev_tail = ch_scr[ts:ts + CONV_HALO, :]
    ch_scr[0:CONV_HALO, :] = jnp.where(s > 0, prev_tail, 0.0)

    _modulated_input(x_ref, mod_ref, g_ref, h_scr)

    def proj(seg_start, c0, width):
        return _dot(h_scr[...], w_in_ref[:, seg_start + c0:seg_start + c0 + width])

    for c0 in range(0, E_A, COL_BLOCK):
        a_h = proj(0 * E_A, c0, COL_BLOCK)
        a_b = proj(1 * E_A, c0, COL_BLOCK)
        a_c = proj(2 * E_A, c0, COL_BLOCK)
        a_z = proj(3 * E_A, c0, COL_BLOCK)
        cols = slice(c0, c0 + COL_BLOCK)
        ch = a_c * a_h
        ch_scr[CONV_HALO:CONV_HALO + ts, cols] = ch
        x_m2 = ch_scr[CONV_HALO - 2:CONV_HALO - 2 + ts, cols]
        x_m1 = ch_scr[CONV_HALO - 1:CONV_HALO - 1 + ts, cols]
        w = conv_ref[:, cols]
        conv = x_m2 * w[0:1] + x_m1 * w[1:2] + ch * w[2:3]
        y_scr[:, cols] = (a_b * conv * _silu(a_z)).astype(_BF16)

    n_chunks = ts // CHUNK
    row = lax.broadcasted_iota(jnp.int32, (CHUNK, CHUNK), 0)
    col = lax.broadcasted_iota(jnp.int32, (CHUNK, CHUNK), 1)
    causal = row >= col
    base_b = 4 * E_A
    for c0 in range(0, E_B, COL_BLOCK):
        b_u = proj(base_b + 0 * E_B, c0, COL_BLOCK)
        b_v = proj(base_b + 1 * E_B, c0, COL_BLOCK)
        b_z = proj(base_b + 2 * E_B, c0, COL_BLOCK)
        mixed_heads = []
        for hh in range(COL_BLOCK // HEAD_DIM):
            head = (c0 + hh * HEAD_DIM) // HEAD_DIM
            lanes = slice(hh * HEAD_DIM, (hh + 1) * HEAD_DIM)
            v = b_v[:, lanes]
            dev = v - jnp.mean(v, axis=-1, keepdims=True)
            var = jnp.mean(dev * dev, axis=-1, keepdims=True)
            glanes = slice(head * HEAD_DIM, (head + 1) * HEAD_DIM)
            vn = dev * lax.rsqrt(var + EPS) * lng_ref[:, glanes] + lnb_ref[:, glanes]
            vn = vn.astype(_BF16)
            rhs = jnp.concatenate(
                [vn[n * CHUNK:(n + 1) * CHUNK] for n in range(n_chunks)], axis=1)
            w_h = jnp.where(causal, sgw_ref[head], 0.0).astype(_BF16)
            mixed = _dot(w_h, rhs) + sgb_ref[head]
            mixed_heads.append(jnp.concatenate(
                [mixed[:, n * CHUNK:(n + 1) * CHUNK] for n in range(n_chunks)], axis=0))
        mixed = jnp.concatenate(mixed_heads, axis=1)
        y_scr[:, E_A + c0:E_A + c0 + COL_BLOCK] = (b_u * mixed * _silu(b_z)).astype(_BF16)

    out = _dot(y_scr[...], w_out_ref[...])
    _residual_out(x_ref, mod_ref, out, final_g_ref, o_ref)


def _odd_kernel(x_ref, mod_ref, g_ref, w_in_ref, pool_w_ref, pscale_ref, w_out_ref,
                *rest, final_norm):
    if final_norm:
        final_g_ref, o_ref, h_scr, y_scr, p_scr = rest
    else:
        final_g_ref = None
        o_ref, h_scr, y_scr, p_scr = rest
    ts = x_ref.shape[0]
    b = pl.program_id(0)
    s = pl.program_id(1)

    @pl.when((b == 0) & (s == 0))
    def _():
        p_scr[ts:ts + POOL_HALO, :] = jnp.zeros((POOL_HALO, E_C), _F32)

    prev_tail = p_scr[ts:ts + POOL_HALO, :]
    p_scr[0:POOL_HALO, :] = jnp.where(s > 0, prev_tail, 0.0)

    _modulated_input(x_ref, mod_ref, g_ref, h_scr)

    pos = s * ts + lax.broadcasted_iota(jnp.int32, (ts, HEAD_DIM), 0)
    for gi, win in enumerate(POOL_WINDOWS):
        cnt = jnp.minimum(pos + 1, win).astype(_F32)
        inv_cnt = jnp.concatenate([1.0 / cnt] * (COL_BLOCK // HEAD_DIM), axis=1)
        for c0 in range(gi * G_C, (gi + 1) * G_C, COL_BLOCK):
            cols = slice(c0, c0 + COL_BLOCK)
            p = _dot(h_scr[...], w_in_ref[:, cols])
            p_scr[POOL_HALO:POOL_HALO + ts, cols] = p
            wsum = p
            for j in range(1, win):
                wsum = wsum + p_scr[POOL_HALO - j:POOL_HALO - j + ts, cols]
            y_scr[:, cols] = (wsum * inv_cnt - p).astype(_BF16)

    for gi in range(len(POOL_WINDOWS)):
        gcols = slice(gi * G_C, (gi + 1) * G_C)
        mixed = _dot(y_scr[:, gcols], pool_w_ref[gi])
        z = _dot(h_scr[...], w_in_ref[:, E_C + gi * G_C:E_C + (gi + 1) * G_C])
        y_scr[:, gcols] = (mixed * pscale_ref[:, gcols] * _silu(z)).astype(_BF16)

    out = _dot(y_scr[...], w_out_ref[...])
    _residual_out(x_ref, mod_ref, out, final_g_ref, o_ref)


def _layer_call(kernel_fn, name, x, mod_i, norm_g_i, weights, weight_specs,
                scratch_shapes, final_g):
    batch, seq, _ = x.shape
    ts = SEQ_TILE
    x_spec = pl.BlockSpec((None, ts, D_MODEL), lambda b, s: (b, s, 0))
    in_specs = [
        x_spec,
        pl.BlockSpec((None, 3, 1, D_MODEL), lambda b, s: (b, 0, 0, 0)),
        _resident((1, D_MODEL), lambda b, s: (0, 0)),
    ] + weight_specs
    args = [x, mod_i, norm_g_i.reshape(1, D_MODEL)] + weights
    if final_g is not None:
        in_specs.append(_resident((1, D_MODEL), lambda b, s: (0, 0)))
        args.append(final_g.reshape(1, D_MODEL))
    return pl.pallas_call(
        functools.partial(kernel_fn, final_norm=final_g is not None),
        grid=(batch, seq // ts),
        in_specs=in_specs,
        out_specs=x_spec,
        out_shape=jax.ShapeDtypeStruct(x.shape, x.dtype),
        scratch_shapes=[
            pltpu.VMEM((ts, D_MODEL), _BF16),
            pltpu.VMEM((ts, E_INNER), _BF16),
        ] + scratch_shapes,
        compiler_params=pltpu.CompilerParams(
            dimension_semantics=("arbitrary", "arbitrary"),
            vmem_limit_bytes=VMEM_LIMIT_BYTES),
        name=name,
    )(*args)


def _even_layer(x, mod_i, norm_g_i, w_in, conv_w, ln_g, ln_b, sgu_w, sgu_b, w_out,
                final_g):
    const2 = lambda b, s: (0, 0)
    const3 = lambda b, s: (0, 0, 0)
    weights = [w_in.astype(_BF16), conv_w, ln_g.reshape(1, E_B), ln_b.reshape(1, E_B),
               sgu_w, sgu_b.reshape(H_B, CHUNK, 1), w_out.astype(_BF16)]
    specs = [
        _resident((D_MODEL, AB_IN), const2),
        _resident((CONV_WIDTH, E_A), const2),
        _resident((1, E_B), const2),
        _resident((1, E_B), const2),
        _resident((H_B, CHUNK, CHUNK), const3),
        _resident((H_B, CHUNK, 1), const3),
        _resident((E_INNER, D_MODEL), const2),
    ]
    scratch = [pltpu.VMEM((SEQ_TILE + CONV_HALO, E_A), _F32)]
    return _layer_call(_even_kernel, "even_layer", x, mod_i, norm_g_i, weights, specs,
                       scratch, final_g)


def _odd_layer(x, mod_i, norm_g_i, w_in, pool_w, pool_scale, w_out, final_g):
    const2 = lambda b, s: (0, 0)
    const3 = lambda b, s: (0, 0, 0)
    weights = [w_in.astype(_BF16), pool_w.astype(_BF16), pool_scale.reshape(1, E_C),
               w_out.astype(_BF16)]
    specs = [
        _resident((D_MODEL, 2 * E_C), const2),
        _resident((len(POOL_WINDOWS), G_C, G_C), const3),
        _resident((1, E_C), const2),
        _resident((E_C, D_MODEL), const2),
    ]
    scratch = [pltpu.VMEM((SEQ_TILE + POOL_HALO, E_C), _F32)]
    return _layer_call(_odd_kernel, "odd_layer", x, mod_i, norm_g_i, weights, specs,
                       scratch, final_g)


def kernel(x, c, norm_g, ada_w, ada_b, ab_w_in, ab_conv_w, ab_ln_g, ab_ln_b, ab_sgu_w,
           ab_sgu_b, ab_w_out, c_w_in, c_pool_w, c_pool_scale, c_w_out, final_g):
    batch = x.shape[0]
    assert x.shape[1] % SEQ_TILE == 0 and SEQ_TILE % CHUNK == 0
    mod = _modulation(c, ada_w, ada_b).reshape(DEPTH, batch, 3, 1, D_MODEL)
    for i in range(DEPTH):
        j = i // 2
        fg = final_g if i == DEPTH - 1 else None
        if i % 2 == 0:
            x = _even_layer(x, mod[i], norm_g[i], ab_w_in[j], ab_conv_w[j], ab_ln_g[j],
                            ab_ln_b[j], ab_sgu_w[j], ab_sgu_b[j], ab_w_out[j], fg)
        else:
            x = _odd_layer(x, mod[i], norm_g[i], c_w_in[j], c_pool_w[j],
                           c_pool_scale[j], c_w_out[j], fg)
    return x
```

```python
import functools

import jax
import jax.numpy as jnp
from jax import lax
from jax.experimental import pallas as pl
from jax.experimental.pallas import tpu as pltpu

D_MODEL = 1024
DEPTH = 4
E_INNER = 2 * D_MODEL
HEAD_DIM = 128
E_A = E_INNER // 2
E_B = E_INNER - E_A
H_B = E_B // HEAD_DIM
CONV_WIDTH = 3
CHUNK = 128
AB_IN = 4 * E_A + 3 * E_B
E_C = E_INNER
POOL_WINDOWS = (2, 4, 8, 16)
G_C = E_C // len(POOL_WINDOWS)
EPS = 1e-6

SEQ_TILE = 512
COL_BLOCK = 256
CONV_HALO = 8
POOL_HALO = 16
MOD_COLS = 1024
VMEM_LIMIT_BYTES = 56 * 1024 * 1024

_F32 = jnp.float32
_BF16 = jnp.bfloat16


_dot = functools.partial(jnp.dot, preferred_element_type=_F32)


def _silu(v):
    return v * jax.nn.sigmoid(v)


def _rmsnorm(x, g):
    return x * lax.rsqrt(jnp.mean(x * x, axis=-1, keepdims=True) + EPS) * g


def _resident(block_shape, index_map):
    return pl.BlockSpec(block_shape, index_map, pipeline_mode=pl.Buffered(1))


def _mod_kernel(c_ref, w_ref, b_ref, o_ref):
    c_act = _silu(c_ref[...]).astype(_BF16)
    o_ref[...] = _dot(c_act, w_ref[...].astype(_BF16)) + b_ref[...]


def _modulation(c, ada_w, ada_b):
    batch = c.shape[0]
    n_col = 3 * D_MODEL
    return pl.pallas_call(
        _mod_kernel,
        grid=(DEPTH, n_col // MOD_COLS),
        in_specs=[
            pl.BlockSpec((batch, D_MODEL), lambda i, j: (0, 0)),
            pl.BlockSpec((None, D_MODEL, MOD_COLS), lambda i, j: (i, 0, j)),
            pl.BlockSpec((None, 1, MOD_COLS), lambda i, j: (i, 0, j)),
        ],
        out_specs=pl.BlockSpec((None, batch, MOD_COLS), lambda i, j: (i, 0, j)),
        out_shape=jax.ShapeDtypeStruct((DEPTH, batch, n_col), _F32),
        compiler_params=pltpu.CompilerParams(
            dimension_semantics=("arbitrary", "arbitrary")),
        name="adaln_modulation",
    )(c, ada_w, ada_b.reshape(DEPTH, 1, n_col))


def _modulated_input(x_ref, mod_ref, g_ref, h_scr):
    x = x_ref[...]
    h = _rmsnorm(x, g_ref[...]) * (1.0 + mod_ref[1]) + mod_ref[0]
    h_scr[...] = h.astype(_BF16)


def _residual_out(x_ref, mod_ref, out, final_g_ref, o_ref):
    res = x_ref[...] + mod_ref[2] * out
    if final_g_ref is not None:
        res = _rmsnorm(res, final_g_ref[...])
    o_ref[...] = res


def _shift_rows(ext, k, halo):
    return pltpu.roll(ext, k, axis=0)[halo:]


def _even_kernel(x_ref, mod_ref, g_ref, w_in_ref, conv_ref, lng_ref, lnb_ref,
                 sgw_ref, sgb_ref, w_out_ref, *rest, final_norm):
    if final_norm:
        final_g_ref, o_ref, h_scr, y_scr, tail_scr = rest
    else:
        final_g_ref = None
        o_ref, h_scr, y_scr, tail_scr = rest
    ts = x_ref.shape[0]
    n_chunks = ts // CHUNK
    b = pl.program_id(0)
    s = pl.program_id(1)

    @pl.when((b == 0) & (s == 0))
    def _():
        tail_scr[...] = jnp.zeros(tail_scr.shape, _F32)

    _modulated_input(x_ref, mod_ref, g_ref, h_scr)

    def proj(seg_start, c0):
        return _dot(h_scr[...], w_in_ref[:, seg_start + c0:seg_start + c0 + COL_BLOCK])

    def a_block(c0):
        cols = slice(c0, c0 + COL_BLOCK)
        a_h = proj(0 * E_A, c0)
        a_c = proj(2 * E_A, c0)
        a_b = proj(1 * E_A, c0)
        a_z = proj(3 * E_A, c0)
        ch = a_c * a_h
        halo = jnp.where(s > 0, tail_scr[:, cols], 0.0)
        tail_scr[:, cols] = ch[ts - CONV_HALO:]
        ext = jnp.concatenate([halo, ch], axis=0)
        w = conv_ref[:, cols]
        conv = (_shift_rows(ext, 2, CONV_HALO) * w[0:1]
                + _shift_rows(ext, 1, CONV_HALO) * w[1:2] + ch * w[2:3])
        y_scr[:, cols] = (a_b * conv * _silu(a_z)).astype(_BF16)

    row = lax.broadcasted_iota(jnp.int32, (CHUNK, CHUNK), 0)
    col = lax.broadcasted_iota(jnp.int32, (CHUNK, CHUNK), 1)
    causal = row >= col
    base_b = 4 * E_A
    heads_per_block = COL_BLOCK // HEAD_DIM

    def ln_block(c0):
        b_v = proj(base_b + 1 * E_B, c0)
        rhs = []
        for hh in range(heads_per_block):
            glanes = slice(c0 + hh * HEAD_DIM, c0 + (hh + 1) * HEAD_DIM)
            v = b_v[:, hh * HEAD_DIM:(hh + 1) * HEAD_DIM]
            dev = v - jnp.mean(v, axis=-1, keepdims=True)
            var = jnp.mean(dev * dev, axis=-1, keepdims=True)
            vn = dev * lax.rsqrt(var + EPS) * lng_ref[:, glanes] + lnb_ref[:, glanes]
            vn = vn.astype(_BF16)
            rhs.append(jnp.concatenate(
                [vn[n * CHUNK:(n + 1) * CHUNK] for n in range(n_chunks)], axis=1))
        return rhs

    def sgu_block(c0, rhs):
        mixed_heads = []
        for hh in range(heads_per_block):
            head = c0 // HEAD_DIM + hh
            w_h = jnp.where(causal, sgw_ref[head], 0.0).astype(_BF16)
            mixed = _dot(w_h, rhs[hh]) + sgb_ref[head]
            mixed_heads.append(jnp.concatenate(
                [mixed[:, n * CHUNK:(n + 1) * CHUNK] for n in range(n_chunks)], axis=0))
        mixed = jnp.concatenate(mixed_heads, axis=1)
        b_u = proj(base_b + 0 * E_B, c0)
        b_z = proj(base_b + 2 * E_B, c0)
        y_scr[:, E_A + c0:E_A + c0 + COL_BLOCK] = (b_u * mixed * _silu(b_z)).astype(_BF16)

    n_blocks = E_B // COL_BLOCK
    rhs = [ln_block(0), ln_block(COL_BLOCK)]
    for k in range(n_blocks):
        a_block(k * COL_BLOCK)
        if k + 2 < n_blocks:
            rhs.append(ln_block((k + 2) * COL_BLOCK))
        sgu_block(k * COL_BLOCK, rhs[k])

    out = None
    for k0 in range(0, E_INNER, COL_BLOCK):
        part = _dot(y_scr[:, k0:k0 + COL_BLOCK], w_out_ref[k0:k0 + COL_BLOCK, :])
        out = part if out is None else out + part
    _residual_out(x_ref, mod_ref, out, final_g_ref, o_ref)


def _fold_kernel(wp_ref, wz_ref, pw_ref, ps_ref, wmix_ref, wz_out_ref):
    def split(v):
        hi = v.astype(_BF16)
        return hi, (v - hi.astype(_F32)).astype(_BF16)

    a_hi, a_lo = split(wp_ref[...])
    b_hi, b_lo = split(pw_ref[...])
    folded = _dot(a_hi, b_hi) + (_dot(a_hi, b_lo) + _dot(a_lo, b_hi))
    wmix_ref[...] = (folded * ps_ref[...]).astype(_BF16)
    wz_out_ref[...] = wz_ref[...].astype(_BF16)


def _fold_pool_weights(c_w_in, c_pool_w, c_pool_scale):
    n_odd = c_w_in.shape[0]
    n_groups = len(POOL_WINDOWS)
    out_sds = jax.ShapeDtypeStruct((n_odd, D_MODEL, E_C), _BF16)
    return pl.pallas_call(
        _fold_kernel,
        grid=(n_odd, n_groups),
        in_specs=[
            pl.BlockSpec((None, D_MODEL, G_C), lambda j, g: (j, 0, g)),
            pl.BlockSpec((None, D_MODEL, G_C), lambda j, g: (j, 0, n_groups + g)),
            pl.BlockSpec((None, None, G_C, G_C), lambda j, g: (j, g, 0, 0)),
            pl.BlockSpec((None, 1, G_C), lambda j, g: (j, 0, g)),
        ],
        out_specs=[
            pl.BlockSpec((None, D_MODEL, G_C), lambda j, g: (j, 0, g)),
            pl.BlockSpec((None, D_MODEL, G_C), lambda j, g: (j, 0, g)),
        ],
        out_shape=[out_sds, out_sds],
        compiler_params=pltpu.CompilerParams(
            dimension_semantics=("arbitrary", "arbitrary")),
        name="fold_pool_weights",
    )(c_w_in, c_w_in, c_pool_w, c_pool_scale.reshape(n_odd, 1, E_C))


def _odd_kernel(x_ref, mod_ref, g_ref, w_mix_ref, w_z_ref, w_out_ref, *rest, final_norm):
    if final_norm:
        final_g_ref, o_ref, h_scr, y_scr, tail_scr, lh_scr = rest
    else:
        final_g_ref = None
        o_ref, h_scr, y_scr, tail_scr, lh_scr = rest
    ts = x_ref.shape[0]
    b = pl.program_id(0)
    s = pl.program_id(1)
    n_groups = len(POOL_WINDOWS)

    @pl.when((b == 0) & (s == 0))
    def _():
        tail_scr[...] = jnp.zeros(tail_scr.shape, _F32)

    h = _rmsnorm(x_ref[...], g_ref[...]) * (1.0 + mod_ref[1]) + mod_ref[0]
    h_scr[...] = h.astype(_BF16)

    halo = jnp.where(s > 0, tail_scr[...], 0.0)
    tail_scr[...] = h[ts - POOL_HALO:]
    pos = s * ts + lax.broadcasted_iota(jnp.int32, (ts, HEAD_DIM), 0)
    acc = jnp.concatenate([halo, h], axis=0)
    k = 1
    for gi, win in enumerate(POOL_WINDOWS):
        while k < win:
            acc = acc + pltpu.roll(acc, k, axis=0)
            k *= 2
        cnt = jnp.minimum(pos + 1, win).astype(_F32)
        inv_cnt = jnp.concatenate([1.0 / cnt] * (D_MODEL // HEAD_DIM), axis=1)
        lh_scr[gi] = (acc[POOL_HALO:] * inv_cnt - h).astype(_BF16)

    def gate_group(gi):
        gcols = slice(gi * G_C, (gi + 1) * G_C)
        return _silu(_dot(h_scr[...], w_z_ref[:, gcols]))

    def mix_group(gi, gate):
        gcols = slice(gi * G_C, (gi + 1) * G_C)
        mixed = _dot(lh_scr[gi], w_mix_ref[:, gcols])
        y_scr[:, gcols] = (mixed * gate).astype(_BF16)

    lead = min(2, n_groups)
    gates = [gate_group(gi) for gi in range(lead)]
    for gi in range(n_groups):
        if gi + lead < n_groups:
            gates.append(gate_group(gi + lead))
        mix_group(gi, gates[gi])
    out = None
    for k0 in range(0, E_C, COL_BLOCK):
        part = _dot(y_scr[:, k0:k0 + COL_BLOCK], w_out_ref[k0:k0 + COL_BLOCK, :])
        out = part if out is None else out + part
    _residual_out(x_ref, mod_ref, out, final_g_ref, o_ref)


def _layer_call(kernel_fn, name, layer, x, mod, norm_g, weights, weight_specs,
                scratch_shapes, final_g):
    batch, seq, _ = x.shape
    ts = SEQ_TILE
    x_spec = pl.BlockSpec((None, ts, D_MODEL), lambda b, s: (b, s, 0))
    in_specs = [
        x_spec,
        pl.BlockSpec((None, None, 3, 1, D_MODEL), lambda b, s: (layer, b, 0, 0, 0)),
        _resident((None, 1, D_MODEL), lambda b, s: (layer, 0, 0)),
    ] + weight_specs
    args = [x, mod, norm_g.reshape(DEPTH, 1, D_MODEL)] + weights
    if final_g is not None:
        in_specs.append(_resident((1, D_MODEL), lambda b, s: (0, 0)))
        args.append(final_g.reshape(1, D_MODEL))
    return pl.pallas_call(
        functools.partial(kernel_fn, final_norm=final_g is not None),
        grid=(batch, seq // ts),
        in_specs=in_specs,
        out_specs=x_spec,
        out_shape=jax.ShapeDtypeStruct(x.shape, x.dtype),
        scratch_shapes=[
            pltpu.VMEM((ts, D_MODEL), _BF16),
            pltpu.VMEM((ts, E_INNER), _BF16),
        ] + scratch_shapes,
        compiler_params=pltpu.CompilerParams(
            dimension_semantics=("arbitrary", "arbitrary"),
            vmem_limit_bytes=VMEM_LIMIT_BYTES),
        name=name,
    )(*args)


def _even_layer(layer, x, mod, norm_g, w_in, conv_w, ln_g, ln_b, sgu_w, sgu_b, w_out,
                final_g):
    j = layer // 2
    n_even = w_in.shape[0]
    sel2 = lambda b, s: (j, 0, 0)
    sel3 = lambda b, s: (j, 0, 0, 0)
    weights = [w_in, conv_w, ln_g.reshape(n_even, 1, E_B), ln_b.reshape(n_even, 1, E_B),
               sgu_w, sgu_b.reshape(n_even, H_B, CHUNK, 1), w_out]
    specs = [
        _resident((None, D_MODEL, AB_IN), sel2),
        _resident((None, CONV_WIDTH, E_A), sel2),
        _resident((None, 1, E_B), sel2),
        _resident((None, 1, E_B), sel2),
        _resident((None, H_B, CHUNK, CHUNK), sel3),
        _resident((None, H_B, CHUNK, 1), sel3),
        _resident((None, E_INNER, D_MODEL), sel2),
    ]
    scratch = [pltpu.VMEM((CONV_HALO, E_A), _F32)]
    return _layer_call(_even_kernel, "even_layer", layer, x, mod, norm_g, weights, specs,
                       scratch, final_g)


def _odd_layer(layer, x, mod, norm_g, w_mix, w_z, w_out, final_g):
    j = layer // 2
    sel2 = lambda b, s: (j, 0, 0)
    weights = [w_mix, w_z, w_out]
    specs = [
        _resident((None, D_MODEL, E_C), sel2),
        _resident((None, D_MODEL, E_C), sel2),
        _resident((None, E_C, D_MODEL), sel2),
    ]
    scratch = [
        pltpu.VMEM((POOL_HALO, D_MODEL), _F32),
        pltpu.VMEM((len(POOL_WINDOWS), SEQ_TILE, D_MODEL), _BF16),
    ]
    return _layer_call(_odd_kernel, "odd_layer", layer, x, mod, norm_g, weights, specs,
                       scratch, final_g)


def kernel(x, c, norm_g, ada_w, ada_b, ab_w_in, ab_conv_w, ab_ln_g, ab_ln_b, ab_sgu_w,
           ab_sgu_b, ab_w_out, c_w_in, c_pool_w, c_pool_scale, c_w_out, final_g):
    batch = x.shape[0]
    assert x.shape[1] % SEQ_TILE == 0 and SEQ_TILE % CHUNK == 0
    mod = _modulation(c, ada_w, ada_b).reshape(DEPTH, batch, 3, 1, D_MODEL)
    ab_w_in_bf, ab_w_out_bf = ab_w_in.astype(_BF16), ab_w_out.astype(_BF16)
    c_w_mix_bf, c_w_z_bf = _fold_pool_weights(c_w_in, c_pool_w, c_pool_scale)
    c_w_out_bf = c_w_out.astype(_BF16)
    for i in range(DEPTH):
        fg = final_g if i == DEPTH - 1 else None
        if i % 2 == 0:
            x = _even_layer(i, x, mod, norm_g, ab_w_in_bf, ab_conv_w, ab_ln_g, ab_ln_b,
                            ab_sgu_w, ab_sgu_b, ab_w_out_bf, fg)
        else:
            x = _odd_layer(i, x, mod, norm_g, c_w_mix_bf, c_w_z_bf, c_w_out_bf, fg)
    return x
```

```python
import functools

import jax
import jax.numpy as jnp
from jax import lax
from jax.experimental import pallas as pl
from jax.experimental.pallas import tpu as pltpu

D_MODEL = 1024
DEPTH = 4
E_INNER = 2 * D_MODEL
HEAD_DIM = 128
E_A = E_INNER // 2
E_B = E_INNER - E_A
H_B = E_B // HEAD_DIM
CONV_WIDTH = 3
CHUNK = 128
AB_IN = 4 * E_A + 3 * E_B
E_C = E_INNER
POOL_WINDOWS = (2, 4, 8, 16)
G_C = E_C // len(POOL_WINDOWS)
EPS = 1e-6

SEQ_BLOCK = 1024
SEQ_TILE = 512
COL_BLOCK = 256
CONV_HALO = 8
POOL_HALO = 16
MOD_COLS = 1024
VMEM_LIMIT_BYTES = 56 * 1024 * 1024

_F32 = jnp.float32
_BF16 = jnp.bfloat16


_dot = functools.partial(jnp.dot, preferred_element_type=_F32)


def _silu(v):
    return v * jax.nn.sigmoid(v)


def _rmsnorm(x, g):
    return x * lax.rsqrt(jnp.mean(x * x, axis=-1, keepdims=True) + EPS) * g


def _resident(block_shape, index_map):
    return pl.BlockSpec(block_shape, index_map, pipeline_mode=pl.Buffered(1))


def _mod_kernel(c_ref, w_ref, b_ref, o_ref):
    c_act = _silu(c_ref[...]).astype(_BF16)
    o_ref[...] = _dot(c_act, w_ref[...].astype(_BF16)) + b_ref[...]


def _modulation(c, ada_w, ada_b):
    batch = c.shape[0]
    n_col = 3 * D_MODEL
    return pl.pallas_call(
        _mod_kernel,
        grid=(DEPTH, n_col // MOD_COLS),
        in_specs=[
            pl.BlockSpec((batch, D_MODEL), lambda i, j: (0, 0)),
            pl.BlockSpec((None, D_MODEL, MOD_COLS), lambda i, j: (i, 0, j)),
            pl.BlockSpec((None, 1, MOD_COLS), lambda i, j: (i, 0, j)),
        ],
        out_specs=pl.BlockSpec((None, batch, MOD_COLS), lambda i, j: (i, 0, j)),
        out_shape=jax.ShapeDtypeStruct((DEPTH, batch, n_col), _F32),
        compiler_params=pltpu.CompilerParams(
            dimension_semantics=("arbitrary", "arbitrary")),
        name="adaln_modulation",
    )(c, ada_w, ada_b.reshape(DEPTH, 1, n_col))


def _modulated_input(x, mod_ref, g_ref):
    return _rmsnorm(x, g_ref[...]) * (1.0 + mod_ref[1]) + mod_ref[0]


def _residual_out(x, mod_ref, out, final_g_ref):
    res = x + mod_ref[2] * out
    if final_g_ref is not None:
        res = _rmsnorm(res, final_g_ref[...])
    return res


def _split_refs(rest, final_norm, n_cast):
    rest = list(rest)
    final_g_ref = rest.pop(0) if final_norm else None
    cast_src = [rest.pop(0) for _ in range(n_cast)]
    o_ref = rest.pop(0)
    cast_dst = [rest.pop(0) for _ in range(n_cast)]
    return final_g_ref, cast_src, o_ref, cast_dst, rest


def _cast_next_weights(cast_src, cast_dst):
    for src, dst in zip(cast_src, cast_dst):
        dst[...] = src[...].astype(_BF16)


def _shift_rows(ext, k, halo):
    return pltpu.roll(ext, k, axis=0)[halo:]


def _tiles(x_ref):
    n_tiles = x_ref.shape[0] // SEQ_TILE
    return [slice(i * SEQ_TILE, (i + 1) * SEQ_TILE) for i in range(n_tiles)]


def _even_kernel(x_ref, mod_ref, g_ref, w_in_ref, conv_ref, lng_ref, lnb_ref,
                 sgw_ref, sgb_ref, w_out_ref, *rest, final_norm, n_cast):
    final_g_ref, cast_src, o_ref, cast_dst, scratch = _split_refs(rest, final_norm, n_cast)
    h_scr, y_scr, tail_scr = scratch
    ts = SEQ_TILE
    n_chunks = ts // CHUNK
    b = pl.program_id(0)
    s = pl.program_id(1)

    @pl.when((b == 0) & (s == 0))
    def _():
        tail_scr[...] = jnp.zeros(tail_scr.shape, _F32)

    row = lax.broadcasted_iota(jnp.int32, (CHUNK, CHUNK), 0)
    col = lax.broadcasted_iota(jnp.int32, (CHUNK, CHUNK), 1)
    causal = row >= col
    base_b = 4 * E_A
    heads_per_block = COL_BLOCK // HEAD_DIM
    n_blocks = E_B // COL_BLOCK

    def mix_tile(ti, rows):
        def proj(seg_start, c0):
            return _dot(h_scr[rows], w_in_ref[:, seg_start + c0:seg_start + c0 + COL_BLOCK])

        def a_block(c0):
            cols = slice(c0, c0 + COL_BLOCK)
            a_h = proj(0 * E_A, c0)
            a_c = proj(2 * E_A, c0)
            a_b = proj(1 * E_A, c0)
            a_z = proj(3 * E_A, c0)
            ch = a_c * a_h
            halo = tail_scr[:, cols]
            if ti == 0:
                halo = jnp.where(s > 0, halo, 0.0)
            tail_scr[:, cols] = ch[ts - CONV_HALO:]
            ext = jnp.concatenate([halo, ch], axis=0)
            w = conv_ref[:, cols]
            conv = (_shift_rows(ext, 2, CONV_HALO) * w[0:1]
                    + _shift_rows(ext, 1, CONV_HALO) * w[1:2] + ch * w[2:3])
            y_scr[rows, cols] = (a_b * conv * _silu(a_z)).astype(_BF16)

        def ln_block(c0):
            b_v = proj(base_b + 1 * E_B, c0)
            rhs = []
            for hh in range(heads_per_block):
                glanes = slice(c0 + hh * HEAD_DIM, c0 + (hh + 1) * HEAD_DIM)
                v = b_v[:, hh * HEAD_DIM:(hh + 1) * HEAD_DIM]
                dev = v - jnp.mean(v, axis=-1, keepdims=True)
                var = jnp.mean(dev * dev, axis=-1, keepdims=True)
                vn = dev * lax.rsqrt(var + EPS) * lng_ref[:, glanes] + lnb_ref[:, glanes]
                vn = vn.astype(_BF16)
                rhs.append(jnp.concatenate(
                    [vn[n * CHUNK:(n + 1) * CHUNK] for n in range(n_chunks)], axis=1))
            return rhs

        def sgu_block(c0, rhs):
            mixed_heads = []
            for hh in range(heads_per_block):
                head = c0 // HEAD_DIM + hh
                w_h = jnp.where(causal, sgw_ref[head], 0.0).astype(_BF16)
                mixed = _dot(w_h, rhs[hh]) + sgb_ref[head]
                mixed_heads.append(jnp.concatenate(
                    [mixed[:, n * CHUNK:(n + 1) * CHUNK] for n in range(n_chunks)], axis=0))
            mixed = jnp.concatenate(mixed_heads, axis=1)
            b_u = proj(base_b + 0 * E_B, c0)
            b_z = proj(base_b + 2 * E_B, c0)
            y_scr[rows, E_A + c0:E_A + c0 + COL_BLOCK] = (
                b_u * mixed * _silu(b_z)).astype(_BF16)

        rhs = [ln_block(0), ln_block(COL_BLOCK)]
        for k in range(n_blocks):
            a_block(k * COL_BLOCK)
            if k + 2 < n_blocks:
                rhs.append(ln_block((k + 2) * COL_BLOCK))
            sgu_block(k * COL_BLOCK, rhs[k])

        out = None
        for k0 in range(0, E_INNER, COL_BLOCK):
            part = _dot(y_scr[rows, k0:k0 + COL_BLOCK], w_out_ref[k0:k0 + COL_BLOCK, :])
            out = part if out is None else out + part
        o_ref[rows] = _residual_out(x_ref[rows], mod_ref, out, final_g_ref)

    tiles = _tiles(x_ref)
    for rows in tiles:
        h_scr[rows] = _modulated_input(x_ref[rows], mod_ref, g_ref).astype(_BF16)
    for ti, rows in enumerate(tiles):
        mix_tile(ti, rows)
    _cast_next_weights(cast_src, cast_dst)


def _fold_kernel(wp_ref, pw_ref, ps_ref, wmix_ref):
    def split(v):
        hi = v.astype(_BF16)
        return hi, (v - hi.astype(_F32)).astype(_BF16)

    a_hi, a_lo = split(wp_ref[...])
    b_hi, b_lo = split(pw_ref[...])
    folded = _dot(a_hi, b_hi) + (_dot(a_hi, b_lo) + _dot(a_lo, b_hi))
    wmix_ref[...] = (folded * ps_ref[...]).astype(_BF16)


def _fold_pool_weights(c_w_in, c_pool_w, c_pool_scale):
    n_odd = c_w_in.shape[0]
    return pl.pallas_call(
        _fold_kernel,
        grid=(n_odd, len(POOL_WINDOWS)),
        in_specs=[
            pl.BlockSpec((None, D_MODEL, G_C), lambda j, g: (j, 0, g)),
            pl.BlockSpec((None, None, G_C, G_C), lambda j, g: (j, g, 0, 0)),
            pl.BlockSpec((None, 1, G_C), lambda j, g: (j, 0, g)),
        ],
        out_specs=pl.BlockSpec((None, D_MODEL, G_C), lambda j, g: (j, 0, g)),
        out_shape=jax.ShapeDtypeStruct((n_odd, D_MODEL, E_C), _BF16),
        compiler_params=pltpu.CompilerParams(
            dimension_semantics=("arbitrary", "arbitrary")),
        name="fold_pool_weights",
    )(c_w_in, c_pool_w, c_pool_scale.reshape(n_odd, 1, E_C))


def _odd_kernel(x_ref, mod_ref, g_ref, w_mix_ref, w_z_ref, w_out_ref, *rest, final_norm,
                n_cast):
    final_g_ref, cast_src, o_ref, cast_dst, scratch = _split_refs(rest, final_norm, n_cast)
    h_scr, y_scr, tail_scr, lh_scr = scratch
    ts = SEQ_TILE
    b = pl.program_id(0)
    s = pl.program_id(1)
    n_groups = len(POOL_WINDOWS)
    tiles = _tiles(x_ref)

    @pl.when((b == 0) & (s == 0))
    def _():
        tail_scr[...] = jnp.zeros(tail_scr.shape, _F32)

    def prepare(ti, rows):
        h = _modulated_input(x_ref[rows], mod_ref, g_ref)
        h_scr[rows] = h.astype(_BF16)
        halo = tail_scr[...]
        if ti == 0:
            halo = jnp.where(s > 0, halo, 0.0)
        tail_scr[...] = h[ts - POOL_HALO:]
        pos = ((s * len(tiles) + ti) * ts
               + lax.broadcasted_iota(jnp.int32, (ts, HEAD_DIM), 0))
        acc = jnp.concatenate([halo, h], axis=0)
        k = 1
        for gi, win in enumerate(POOL_WINDOWS):
            while k < win:
                acc = acc + pltpu.roll(acc, k, axis=0)
                k *= 2
            cnt = jnp.minimum(pos + 1, win).astype(_F32)
            inv_cnt = jnp.concatenate([1.0 / cnt] * (D_MODEL // HEAD_DIM), axis=1)
            lh_scr[gi, rows] = (acc[POOL_HALO:] * inv_cnt - h).astype(_BF16)

    def mix_tile(rows):
        def gate_group(gi):
            gcols = slice(gi * G_C, (gi + 1) * G_C)
            return _silu(_dot(h_scr[rows], w_z_ref[:, gcols]))

        def mix_group(gi, gate):
            gcols = slice(gi * G_C, (gi + 1) * G_C)
            mixed = _dot(lh_scr[gi, rows], w_mix_ref[:, gcols])
            y_scr[rows, gcols] = (mixed * gate).astype(_BF16)

        lead = min(2, n_groups)
        gates = [gate_group(gi) for gi in range(lead)]
        for gi in range(n_groups):
            if gi + lead < n_groups:
                gates.append(gate_group(gi + lead))
            mix_group(gi, gates[gi])
        out = None
        for k0 in range(0, E_C, COL_BLOCK):
            part = _dot(y_scr[rows, k0:k0 + COL_BLOCK], w_out_ref[k0:k0 + COL_BLOCK, :])
            out = part if out is None else out + part
        o_ref[rows] = _residual_out(x_ref[rows], mod_ref, out, final_g_ref)

    for ti, rows in enumerate(tiles):
        prepare(ti, rows)
    for rows in tiles:
        mix_tile(rows)
    _cast_next_weights(cast_src, cast_dst)


def _layer_call(kernel_fn, name, layer, x, mod, norm_g, weights, weight_specs,
                scratch_shapes, final_g, cast_jobs):
    batch, seq, _ = x.shape
    n_seq = seq // SEQ_BLOCK
    n_steps = batch * n_seq
    x_spec = pl.BlockSpec((None, SEQ_BLOCK, D_MODEL), lambda b, s: (b, s, 0))
    in_specs = [
        x_spec,
        pl.BlockSpec((None, None, 3, 1, D_MODEL), lambda b, s: (layer, b, 0, 0, 0)),
        _resident((None, 1, D_MODEL), lambda b, s: (layer, 0, 0)),
    ] + weight_specs
    args = [x, mod, norm_g.reshape(DEPTH, 1, D_MODEL)] + weights
    if final_g is not None:
        in_specs.append(_resident((1, D_MODEL), lambda b, s: (0, 0)))
        args.append(final_g.reshape(1, D_MODEL))
    out_specs = [x_spec]
    out_shape = [jax.ShapeDtypeStruct(x.shape, x.dtype)]
    for src, j, col_block, n_cols in cast_jobs:
        n_rows = src.shape[1]
        assert n_rows % n_steps == 0
        rows = n_rows // n_steps
        in_specs.append(pl.BlockSpec(
            (None, rows, n_cols),
            lambda b, s, j=j, cb=col_block: (j, b * n_seq + s, cb)))
        args.append(src)
        out_specs.append(pl.BlockSpec((rows, n_cols), lambda b, s: (b * n_seq + s, 0)))
        out_shape.append(jax.ShapeDtypeStruct((n_rows, n_cols), _BF16))
    return pl.pallas_call(
        functools.partial(kernel_fn, final_norm=final_g is not None,
                          n_cast=len(cast_jobs)),
        grid=(batch, n_seq),
        in_specs=in_specs,
        out_specs=out_specs,
        out_shape=out_shape,
        scratch_shapes=[
            pltpu.VMEM((SEQ_BLOCK, D_MODEL), _BF16),
            pltpu.VMEM((SEQ_BLOCK, E_INNER), _BF16),
        ] + scratch_shapes,
        compiler_params=pltpu.CompilerParams(
            dimension_semantics=("arbitrary", "arbitrary"),
            vmem_limit_bytes=VMEM_LIMIT_BYTES),
        name=name,
    )(*args)


def _even_layer(layer, x, mod, norm_g, w_in, conv_w, ln_g, ln_b, sgu_w, sgu_b, w_out,
                final_g, cast_jobs):
    j = layer // 2
    n_even = conv_w.shape[0]
    const2 = lambda b, s: (0, 0)
    sel2 = lambda b, s: (j, 0, 0)
    sel3 = lambda b, s: (j, 0, 0, 0)
    weights = [w_in, conv_w, ln_g.reshape(n_even, 1, E_B), ln_b.reshape(n_even, 1, E_B),
               sgu_w, sgu_b.reshape(n_even, H_B, CHUNK, 1), w_out]
    specs = [
        _resident((D_MODEL, AB_IN), const2),
        _resident((None, CONV_WIDTH, E_A), sel2),
        _resident((None, 1, E_B), sel2),
        _resident((None, 1, E_B), sel2),
        _resident((None, H_B, CHUNK, CHUNK), sel3),
        _resident((None, H_B, CHUNK, 1), sel3),
        _resident((E_INNER, D_MODEL), const2),
    ]
    scratch = [pltpu.VMEM((CONV_HALO, E_A), _F32)]
    return _layer_call(_even_kernel, "even_layer", layer, x, mod, norm_g, weights, specs,
                       scratch, final_g, cast_jobs)


def _odd_layer(layer, x, mod, norm_g, w_mix, w_z, w_out, final_g, cast_jobs):
    j = layer // 2
    const2 = lambda b, s: (0, 0)
    weights = [w_mix, w_z, w_out]
    specs = [
        _resident((None, D_MODEL, E_C), lambda b, s: (j, 0, 0)),
        _resident((D_MODEL, E_C), const2),
        _resident((E_C, D_MODEL), const2),
    ]
    scratch = [
        pltpu.VMEM((POOL_HALO, D_MODEL), _F32),
        pltpu.VMEM((len(POOL_WINDOWS), SEQ_BLOCK, D_MODEL), _BF16),
    ]
    return _layer_call(_odd_kernel, "odd_layer", layer, x, mod, norm_g, weights, specs,
                       scratch, final_g, cast_jobs)


def kernel(x, c, norm_g, ada_w, ada_b, ab_w_in, ab_conv_w, ab_ln_g, ab_ln_b, ab_sgu_w,
           ab_sgu_b, ab_w_out, c_w_in, c_pool_w, c_pool_scale, c_w_out, final_g):
    batch = x.shape[0]
    assert x.shape[1] % SEQ_BLOCK == 0 and SEQ_BLOCK % SEQ_TILE == 0
    assert SEQ_TILE % CHUNK == 0
    mod = _modulation(c, ada_w, ada_b).reshape(DEPTH, batch, 3, 1, D_MODEL)
    c_w_mix = _fold_pool_weights(c_w_in, c_pool_w, c_pool_scale)
    big_w = [ab_w_in[0].astype(_BF16), ab_w_out[0].astype(_BF16)]
    for i in range(DEPTH):
        fg = final_g if i == DEPTH - 1 else None
        nxt = (i + 1) // 2
        if i + 1 == DEPTH:
            cast_jobs = []
        elif i % 2 == 0:
            cast_jobs = [(c_w_in, nxt, 1, E_C), (c_w_out, nxt, 0, D_MODEL)]
        else:
            cast_jobs = [(ab_w_in, nxt, 0, AB_IN), (ab_w_out, nxt, 0, D_MODEL)]
        if i % 2 == 0:
            x, *big_w = _even_layer(i, x, mod, norm_g, big_w[0], ab_conv_w, ab_ln_g,
                                    ab_ln_b, ab_sgu_w, ab_sgu_b, big_w[1], fg, cast_jobs)
        else:
            x, *big_w = _odd_layer(i, x, mod, norm_g, c_w_mix, big_w[0], big_w[1], fg,
                                   cast_jobs)
    return x
```

```python
import functools

import jax
import jax.numpy as jnp
from jax import lax
from jax.experimental import pallas as pl
from jax.experimental.pallas import tpu as pltpu

D_MODEL = 1024
DEPTH = 4
E_INNER = 2 * D_MODEL
HEAD_DIM = 128
E_A = E_INNER // 2
E_B = E_INNER - E_A
H_B = E_B // HEAD_DIM
CONV_WIDTH = 3
CHUNK = 128
AB_IN = 4 * E_A + 3 * E_B
E_C = E_INNER
POOL_WINDOWS = (2, 4, 8, 16)
G_C = E_C // len(POOL_WINDOWS)
EPS = 1e-6

SEQ_BLOCK = 1024
SEQ_TILE = 512
COL_BLOCK = 256
CONV_HALO = 8
POOL_HALO = 16
MOD_COLS = 1024
VMEM_LIMIT_BYTES = 56 * 1024 * 1024

_F32 = jnp.float32
_BF16 = jnp.bfloat16


_dot = functools.partial(jnp.dot, preferred_element_type=_F32)


def _silu(v):
    return v * jax.nn.sigmoid(v)


def _rmsnorm(x, g):
    return x * lax.rsqrt(jnp.mean(x * x, axis=-1, keepdims=True) + EPS) * g


def _resident(block_shape, index_map):
    return pl.BlockSpec(block_shape, index_map, pipeline_mode=pl.Buffered(1))


def _mod_kernel(c_ref, w_ref, b_ref, o_ref):
    c_act = _silu(c_ref[...]).astype(_BF16)
    o_ref[...] = _dot(c_act, w_ref[...].astype(_BF16)) + b_ref[...]


def _modulation(c, ada_w, ada_b):
    batch = c.shape[0]
    n_col = 3 * D_MODEL
    return pl.pallas_call(
        _mod_kernel,
        grid=(DEPTH, n_col // MOD_COLS),
        in_specs=[
            pl.BlockSpec((batch, D_MODEL), lambda i, j: (0, 0)),
            pl.BlockSpec((None, D_MODEL, MOD_COLS), lambda i, j: (i, 0, j)),
            pl.BlockSpec((None, 1, MOD_COLS), lambda i, j: (i, 0, j)),
        ],
        out_specs=pl.BlockSpec((None, batch, MOD_COLS), lambda i, j: (i, 0, j)),
        out_shape=jax.ShapeDtypeStruct((DEPTH, batch, n_col), _F32),
        compiler_params=pltpu.CompilerParams(
            dimension_semantics=("arbitrary", "arbitrary")),
        name="adaln_modulation",
    )(c, ada_w, ada_b.reshape(DEPTH, 1, n_col))


def _modulated_input(x, mod_ref, g_ref):
    return _rmsnorm(x, g_ref[...]) * (1.0 + mod_ref[1]) + mod_ref[0]


def _residual_out(x, mod_ref, out, final_g_ref):
    res = x + mod_ref[2] * out
    if final_g_ref is not None:
        res = _rmsnorm(res, final_g_ref[...])
    return res


def _split_refs(rest, final_norm, n_cast):
    rest = list(rest)
    final_g_ref = rest.pop(0) if final_norm else None
    cast_src = [rest.pop(0) for _ in range(n_cast)]
    o_ref = rest.pop(0)
    cast_dst = [rest.pop(0) for _ in range(n_cast)]
    return final_g_ref, cast_src, o_ref, cast_dst, rest


def _pack_weight(w):
    return pltpu.bitcast(w.astype(_BF16), jnp.uint32)


def _weight(w_ref, k0, k1, cols):
    return pltpu.bitcast(w_ref[k0 // 2:k1 // 2, cols], _BF16)


def _cast_next_weights(cast_src, cast_dst):
    for src, dst in zip(cast_src, cast_dst):
        dst[...] = _pack_weight(src[...])


def _shift_rows(ext, k, halo):
    return pltpu.roll(ext, k, axis=0)[halo:]


def _tiles(x_ref):
    n_tiles = x_ref.shape[0] // SEQ_TILE
    return [slice(i * SEQ_TILE, (i + 1) * SEQ_TILE) for i in range(n_tiles)]


def _run_tiles(stages):
    stages[0][0]()
    for ti, (_, body, finish) in enumerate(stages):
        body()
        if ti + 1 < len(stages):
            stages[ti + 1][0]()
        finish()


def _even_kernel(x_ref, mod_ref, g_ref, w_in_ref, conv_ref, lng_ref, lnb_ref,
                 sgw_ref, sgb_ref, w_out_ref, *rest, final_norm, n_cast):
    final_g_ref, cast_src, o_ref, cast_dst, scratch = _split_refs(rest, final_norm, n_cast)
    h_scr, y_scr, tail_scr = scratch
    ts = SEQ_TILE
    n_chunks = ts // CHUNK
    b = pl.program_id(0)
    s = pl.program_id(1)

    @pl.when((b == 0) & (s == 0))
    def _():
        tail_scr[...] = jnp.zeros(tail_scr.shape, _F32)

    row = lax.broadcasted_iota(jnp.int32, (CHUNK, CHUNK), 0)
    col = lax.broadcasted_iota(jnp.int32, (CHUNK, CHUNK), 1)
    causal = row >= col
    base_b = 4 * E_A
    heads_per_block = COL_BLOCK // HEAD_DIM
    n_blocks = E_B // COL_BLOCK

    def mix_tile(ti, rows):
        def proj(seg_start, c0, width=COL_BLOCK):
            cols = slice(seg_start + c0, seg_start + c0 + width)
            return _dot(h_scr[rows], _weight(w_in_ref, 0, D_MODEL, cols))

        def a_block(c0):
            cols = slice(c0, c0 + COL_BLOCK)
            a_h = proj(0 * E_A, c0)
            a_c = proj(2 * E_A, c0)
            a_b = proj(1 * E_A, c0)
            a_z = proj(3 * E_A, c0)
            ch = a_c * a_h
            halo = tail_scr[:, cols]
            if ti == 0:
                halo = jnp.where(s > 0, halo, 0.0)
            tail_scr[:, cols] = ch[ts - CONV_HALO:]
            ext = jnp.concatenate([halo, ch], axis=0)
            w = conv_ref[:, cols]
            conv = (_shift_rows(ext, 2, CONV_HALO) * w[0:1]
                    + _shift_rows(ext, 1, CONV_HALO) * w[1:2] + ch * w[2:3])
            y_scr[rows, cols] = (a_b * conv * _silu(a_z)).astype(_BF16)

        def ln_blocks(c0):
            b_v = proj(base_b + 1 * E_B, c0, 2 * COL_BLOCK)
            rhs = []
            for hh in range(2 * heads_per_block):
                glanes = slice(c0 + hh * HEAD_DIM, c0 + (hh + 1) * HEAD_DIM)
                v = b_v[:, hh * HEAD_DIM:(hh + 1) * HEAD_DIM]
                dev = v - jnp.mean(v, axis=-1, keepdims=True)
                var = jnp.mean(dev * dev, axis=-1, keepdims=True)
                vn = dev * lax.rsqrt(var + EPS) * lng_ref[:, glanes] + lnb_ref[:, glanes]
                vn = vn.astype(_BF16)
                rhs.append(jnp.concatenate(
                    [vn[n * CHUNK:(n + 1) * CHUNK] for n in range(n_chunks)], axis=1))
            return [rhs[:heads_per_block], rhs[heads_per_block:]]

        def sgu_block(c0, rhs):
            mixed_heads = []
            for hh in range(heads_per_block):
                head = c0 // HEAD_DIM + hh
                w_h = jnp.where(causal, sgw_ref[head], 0.0).astype(_BF16)
                mixed = _dot(w_h, rhs[hh]) + sgb_ref[head]
                mixed_heads.append(jnp.concatenate(
                    [mixed[:, n * CHUNK:(n + 1) * CHUNK] for n in range(n_chunks)], axis=0))
            mixed = jnp.concatenate(mixed_heads, axis=1)
            b_u = proj(base_b + 0 * E_B, c0)
            b_z = proj(base_b + 2 * E_B, c0)
            y_scr[rows, E_A + c0:E_A + c0 + COL_BLOCK] = (
                b_u * mixed * _silu(b_z)).astype(_BF16)

        rhs = []

        def head():
            h_scr[rows] = _modulated_input(x_ref[rows], mod_ref, g_ref).astype(_BF16)
            rhs.extend(ln_blocks(0))

        def body():
            for k in range(n_blocks):
                a_block(k * COL_BLOCK)
                if k % 2 == 0 and k + 2 < n_blocks:
                    rhs.extend(ln_blocks((k + 2) * COL_BLOCK))
                sgu_block(k * COL_BLOCK, rhs[k])

        def finish():
            out = None
            for k0 in range(0, E_INNER, COL_BLOCK):
                part = _dot(y_scr[rows, k0:k0 + COL_BLOCK],
                            _weight(w_out_ref, k0, k0 + COL_BLOCK, slice(None)))
                out = part if out is None else out + part
            o_ref[rows] = _residual_out(x_ref[rows], mod_ref, out, final_g_ref)

        return head, body, finish

    _run_tiles([mix_tile(ti, rows) for ti, rows in enumerate(_tiles(x_ref))])
    _cast_next_weights(cast_src, cast_dst)


def _fold_kernel(wp_ref, pw_ref, ps_ref, *rest):
    n_cast = (len(rest) - 1) // 2
    cast_src, wmix_ref, cast_dst = rest[:n_cast], rest[n_cast], rest[n_cast + 1:]

    def split(v):
        hi = v.astype(_BF16)
        return hi, (v - hi.astype(_F32)).astype(_BF16)

    a_hi, a_lo = split(wp_ref[...])
    b_hi, b_lo = split(pw_ref[...])
    folded = _dot(a_hi, b_hi) + (_dot(a_hi, b_lo) + _dot(a_lo, b_hi))
    wmix_ref[...] = _pack_weight(folded * ps_ref[...])
    _cast_next_weights(cast_src, cast_dst)


def _fold_pool_weights(c_w_in, c_pool_w, c_pool_scale, cast_jobs):
    n_odd = c_w_in.shape[0]
    n_groups = len(POOL_WINDOWS)
    c_in, c_args, c_out, c_shape = _cast_job_specs(
        cast_jobs, n_odd * n_groups, lambda j, g: j * n_groups + g)
    return pl.pallas_call(
        _fold_kernel,
        grid=(n_odd, n_groups),
        in_specs=[
            pl.BlockSpec((None, D_MODEL, G_C), lambda j, g: (j, 0, g)),
            pl.BlockSpec((None, None, G_C, G_C), lambda j, g: (j, g, 0, 0)),
            pl.BlockSpec((None, 1, G_C), lambda j, g: (j, 0, g)),
        ] + c_in,
        out_specs=[pl.BlockSpec((None, D_MODEL // 2, G_C), lambda j, g: (j, 0, g))] + c_out,
        out_shape=[jax.ShapeDtypeStruct((n_odd, D_MODEL // 2, E_C), jnp.uint32)] + c_shape,
        compiler_params=pltpu.CompilerParams(
            dimension_semantics=("arbitrary", "arbitrary"),
            vmem_limit_bytes=VMEM_LIMIT_BYTES),
        name="fold_pool_weights",
    )(c_w_in, c_pool_w, c_pool_scale.reshape(n_odd, 1, E_C), *c_args)


def _odd_kernel(x_ref, mod_ref, g_ref, w_mix_ref, w_z_ref, w_out_ref, *rest, final_norm,
                n_cast):
    final_g_ref, cast_src, o_ref, cast_dst, scratch = _split_refs(rest, final_norm, n_cast)
    h_scr, y_scr, tail_scr, lh_scr = scratch
    ts = SEQ_TILE
    b = pl.program_id(0)
    s = pl.program_id(1)
    n_groups = len(POOL_WINDOWS)
    tiles = _tiles(x_ref)

    @pl.when((b == 0) & (s == 0))
    def _():
        tail_scr[...] = jnp.zeros(tail_scr.shape, _F32)

    def prepare(ti, rows):
        h = _modulated_input(x_ref[rows], mod_ref, g_ref)
        h_scr[rows] = h.astype(_BF16)
        halo = tail_scr[...]
        if ti == 0:
            halo = jnp.where(s > 0, halo, 0.0)
        tail_scr[...] = h[ts - POOL_HALO:]
        pos = ((s * len(tiles) + ti) * ts
               + lax.broadcasted_iota(jnp.int32, (ts, HEAD_DIM), 0))
        acc = jnp.concatenate([halo, h], axis=0)
        k = 1
        for gi, win in enumerate(POOL_WINDOWS):
            while k < win:
                acc = acc + pltpu.roll(acc, k, axis=0)
                k *= 2
            cnt = jnp.minimum(pos + 1, win).astype(_F32)
            inv_cnt = jnp.concatenate([1.0 / cnt] * (D_MODEL // HEAD_DIM), axis=1)
            lh_scr[gi, rows] = (acc[POOL_HALO:] * inv_cnt - h).astype(_BF16)

    def mix_tile(ti, rows):
        def gate_group(gi):
            gcols = slice(gi * G_C, (gi + 1) * G_C)
            return _silu(_dot(h_scr[rows], _weight(w_z_ref, 0, D_MODEL, gcols)))

        def mix_group(gi, gate):
            gcols = slice(gi * G_C, (gi + 1) * G_C)
            mixed = _dot(lh_scr[gi, rows], _weight(w_mix_ref, 0, D_MODEL, gcols))
            y_scr[rows, gcols] = (mixed * gate).astype(_BF16)

        lead = min(2, n_groups)
        gates = []

        def head():
            prepare(ti, rows)
            gates.extend(gate_group(gi) for gi in range(lead))
            mix_group(0, gates[0])

        def body():
            for gi in range(1, n_groups):
                if gi - 1 + lead < n_groups:
                    gates.append(gate_group(gi - 1 + lead))
                mix_group(gi, gates[gi])

        def finish():
            out = None
            for k0 in range(0, E_C, COL_BLOCK):
                part = _dot(y_scr[rows, k0:k0 + COL_BLOCK],
                            _weight(w_out_ref, k0, k0 + COL_BLOCK, slice(None)))
                out = part if out is None else out + part
            o_ref[rows] = _residual_out(x_ref[rows], mod_ref, out, final_g_ref)

        return head, body, finish

    _run_tiles([mix_tile(ti, rows) for ti, rows in enumerate(tiles)])
    _cast_next_weights(cast_src, cast_dst)


def _cast_job_specs(cast_jobs, n_steps, step_of):
    in_specs, args, out_specs, out_shape = [], [], [], []
    for src, j, col_block, n_cols in cast_jobs:
        n_rows = src.shape[1]
        rows = n_rows // n_steps
        assert rows * n_steps == n_rows and rows % 16 == 0
        in_specs.append(pl.BlockSpec(
            (None, rows, n_cols),
            lambda *g, j=j, cb=col_block: (j, step_of(*g), cb)))
        args.append(src)
        out_specs.append(pl.BlockSpec((rows // 2, n_cols), lambda *g: (step_of(*g), 0)))
        out_shape.append(jax.ShapeDtypeStruct((n_rows // 2, n_cols), jnp.uint32))
    return in_specs, args, out_specs, out_shape


def _layer_call(kernel_fn, name, layer, x, mod, norm_g, weights, weight_specs,
                scratch_shapes, final_g, cast_jobs):
    batch, seq, _ = x.shape
    n_seq = seq // SEQ_BLOCK
    n_steps = batch * n_seq
    x_spec = pl.BlockSpec((None, SEQ_BLOCK, D_MODEL), lambda b, s: (b, s, 0))
    in_specs = [
        x_spec,
        pl.BlockSpec((None, None, 3, 1, D_MODEL), lambda b, s: (layer, b, 0, 0, 0)),
        _resident((None, 1, D_MODEL), lambda b, s: (layer, 0, 0)),
    ] + weight_specs
    args = [x, mod, norm_g.reshape(DEPTH, 1, D_MODEL)] + weights
    if final_g is not None:
        in_specs.append(_resident((1, D_MODEL), lambda b, s: (0, 0)))
        args.append(final_g.reshape(1, D_MODEL))
    c_in, c_args, c_out, c_shape = _cast_job_specs(
        cast_jobs, n_steps, lambda b, s: b * n_seq + s)
    in_specs += c_in
    args += c_args
    out_specs = [x_spec] + c_out
    out_shape = [jax.ShapeDtypeStruct(x.shape, x.dtype)] + c_shape
    return pl.pallas_call(
        functools.partial(kernel_fn, final_norm=final_g is not None,
                          n_cast=len(cast_jobs)),
        grid=(batch, n_seq),
        in_specs=in_specs,
        out_specs=out_specs,
        out_shape=out_shape,
        scratch_shapes=[
            pltpu.VMEM((SEQ_BLOCK, D_MODEL), _BF16),
            pltpu.VMEM((SEQ_BLOCK, E_INNER), _BF16),
        ] + scratch_shapes,
        compiler_params=pltpu.CompilerParams(
            dimension_semantics=("arbitrary", "arbitrary"),
            vmem_limit_bytes=VMEM_LIMIT_BYTES),
        name=name,
    )(*args)


def _even_layer(layer, x, mod, norm_g, w_in, conv_w, ln_g, ln_b, sgu_w, sgu_b, w_out,
                final_g, cast_jobs):
    j = layer // 2
    n_even = conv_w.shape[0]
    const2 = lambda b, s: (0, 0)
    sel2 = lambda b, s: (j, 0, 0)
    sel3 = lambda b, s: (j, 0, 0, 0)
    weights = [w_in, conv_w, ln_g.reshape(n_even, 1, E_B), ln_b.reshape(n_even, 1, E_B),
               sgu_w, sgu_b.reshape(n_even, H_B, CHUNK, 1), w_out]
    specs = [
        _resident((D_MODEL // 2, AB_IN), const2),
        _resident((None, CONV_WIDTH, E_A), sel2),
        _resident((None, 1, E_B), sel2),
        _resident((None, 1, E_B), sel2),
        _resident((None, H_B, CHUNK, CHUNK), sel3),
        _resident((None, H_B, CHUNK, 1), sel3),
        _resident((E_INNER // 2, D_MODEL), const2),
    ]
    scratch = [pltpu.VMEM((CONV_HALO, E_A), _F32)]
    return _layer_call(_even_kernel, "even_layer", layer, x, mod, norm_g, weights, specs,
                       scratch, final_g, cast_jobs)


def _odd_layer(layer, x, mod, norm_g, w_mix, w_z, w_out, final_g, cast_jobs):
    j = layer // 2
    const2 = lambda b, s: (0, 0)
    weights = [w_mix, w_z, w_out]
    specs = [
        _resident((None, D_MODEL // 2, E_C), lambda b, s: (j, 0, 0)),
        _resident((D_MODEL // 2, E_C), const2),
        _resident((E_C // 2, D_MODEL), const2),
    ]
    scratch = [
        pltpu.VMEM((POOL_HALO, D_MODEL), _F32),
        pltpu.VMEM((len(POOL_WINDOWS), SEQ_BLOCK, D_MODEL), _BF16),
    ]
    return _layer_call(_odd_kernel, "odd_layer", layer, x, mod, norm_g, weights, specs,
                       scratch, final_g, cast_jobs)


def kernel(x, c, norm_g, ada_w, ada_b, ab_w_in, ab_conv_w, ab_ln_g, ab_ln_b, ab_sgu_w,
           ab_sgu_b, ab_w_out, c_w_in, c_pool_w, c_pool_scale, c_w_out, final_g):
    batch = x.shape[0]
    assert x.shape[1] % SEQ_BLOCK == 0 and SEQ_BLOCK % SEQ_TILE == 0
    assert SEQ_TILE % CHUNK == 0
    mod = _modulation(c, ada_w, ada_b).reshape(DEPTH, batch, 3, 1, D_MODEL)
    c_w_mix, *big_w = _fold_pool_weights(
        c_w_in, c_pool_w, c_pool_scale,
        [(ab_w_in, 0, 0, AB_IN), (ab_w_out, 0, 0, D_MODEL)])
    for i in range(DEPTH):
        fg = final_g if i == DEPTH - 1 else None
        nxt = (i + 1) // 2
        if i + 1 == DEPTH:
            cast_jobs = []
        elif i % 2 == 0:
            cast_jobs = [(c_w_in, nxt, 1, E_C), (c_w_out, nxt, 0, D_MODEL)]
        else:
            cast_jobs = [(ab_w_in, nxt, 0, AB_IN), (ab_w_out, nxt, 0, D_MODEL)]
        if i % 2 == 0:
            x, *big_w = _even_layer(i, x, mod, norm_g, big_w[0], ab_conv_w, ab_ln_g,
                                    ab_ln_b, ab_sgu_w, ab_sgu_b, big_w[1], fg, cast_jobs)
        else:
            x, *big_w = _odd_layer(i, x, mod, norm_g, c_w_mix, big_w[0], big_w[1], fg,
                                   cast_jobs)
    return x
```

```python
import functools

import jax
import jax.numpy as jnp
from jax import lax
from jax.experimental import pallas as pl
from jax.experimental.pallas import tpu as pltpu

D_MODEL = 1024
DEPTH = 4
E_INNER = 2 * D_MODEL
HEAD_DIM = 128
E_A = E_INNER // 2
E_B = E_INNER - E_A
H_B = E_B // HEAD_DIM
CONV_WIDTH = 3
CHUNK = 128
AB_IN = 4 * E_A + 3 * E_B
E_C = E_INNER
POOL_WINDOWS = (2, 4, 8, 16)
G_C = E_C // len(POOL_WINDOWS)
EPS = 1e-6

SEQ_BLOCK = 1024
SEQ_TILE = 512
COL_BLOCK = 256
CONV_HALO = 8
POOL_HALO = 16
VMEM_LIMIT_BYTES = 56 * 1024 * 1024

_F32 = jnp.float32
_BF16 = jnp.bfloat16


_dot = functools.partial(jnp.dot, preferred_element_type=_F32)


def _silu(v):
    return v * jax.nn.sigmoid(v)


def _rmsnorm(x, g):
    return x * lax.rsqrt(jnp.mean(x * x, axis=-1, keepdims=True) + EPS) * g


def _resident(block_shape, index_map):
    return pl.BlockSpec(block_shape, index_map, pipeline_mode=pl.Buffered(1))


def _mod_kernel(c_ref, w_ref, b_ref, o_ref):
    c_act = _silu(c_ref[...]).astype(_BF16)
    bias = b_ref[pl.ds(pl.program_id(0), 1), :]
    o_ref[...] = _dot(c_act, w_ref[...].astype(_BF16)) + bias


def _modulation(c, ada_w, ada_b):
    batch = c.shape[0]
    return pl.pallas_call(
        _mod_kernel,
        grid=(DEPTH, 3),
        in_specs=[
            pl.BlockSpec((batch, D_MODEL), lambda i, j: (0, 0)),
            pl.BlockSpec((None, D_MODEL, D_MODEL), lambda i, j: (i, 0, j)),
            pl.BlockSpec((DEPTH, D_MODEL), lambda i, j: (0, j)),
        ],
        out_specs=pl.BlockSpec((None, None, batch, D_MODEL), lambda i, j: (i, j, 0, 0)),
        out_shape=jax.ShapeDtypeStruct((DEPTH, 3, batch, D_MODEL), _F32),
        compiler_params=pltpu.CompilerParams(
            dimension_semantics=("arbitrary", "arbitrary")),
        name="adaln_modulation",
    )(c, ada_w, ada_b)


def _layer_rows(mod_ref, g_ref, final_g_ref, layer):
    b = pl.program_id(0)
    mod = [mod_ref[k, pl.ds(b, 1), :] for k in range(3)]
    final_g = None if final_g_ref is None else final_g_ref[...].reshape(1, D_MODEL)
    return g_ref[layer:layer + 1, :], mod, final_g


def _modulated_input(x, g, mod):
    return _rmsnorm(x, g) * (1.0 + mod[1]) + mod[0]


def _residual_out(x, mod, out, final_g):
    res = x + mod[2] * out
    if final_g is not None:
        res = _rmsnorm(res, final_g)
    return res


def _split_refs(rest, final_norm, n_cast):
    rest = list(rest)
    final_g_ref = rest.pop(0) if final_norm else None
    cast_src = [rest.pop(0) for _ in range(n_cast)]
    o_ref = rest.pop(0)
    cast_dst = [rest.pop(0) for _ in range(n_cast)]
    return final_g_ref, cast_src, o_ref, cast_dst, rest


def _pack_weight(w):
    return pltpu.bitcast(w.astype(_BF16), jnp.uint32)


def _weight(w_ref, k0, k1, cols):
    return pltpu.bitcast(w_ref[k0 // 2:k1 // 2, cols], _BF16)


def _cast_next_weights(cast_src, cast_dst):
    for src, dst in zip(cast_src, cast_dst):
        dst[...] = _pack_weight(src[...])


def _shift_rows(ext, k, halo):
    return pltpu.roll(ext, k, axis=0)[halo:]


def _tiles(x_ref):
    n_tiles = x_ref.shape[0] // SEQ_TILE
    return [slice(i * SEQ_TILE, (i + 1) * SEQ_TILE) for i in range(n_tiles)]


def _run_tiles(stages):
    stages[0][0]()
    for ti, (_, body, finish) in enumerate(stages):
        body()
        if ti + 1 < len(stages):
            stages[ti + 1][0]()
        finish()


def _even_kernel(x_ref, mod_ref, g_ref, w_in_ref, conv_ref, lng_ref, lnb_ref,
                 sgw_ref, sgb_ref, w_out_ref, *rest, layer, final_norm, n_cast):
    final_g_ref, cast_src, o_ref, cast_dst, scratch = _split_refs(rest, final_norm, n_cast)
    h_scr, y_scr, tail_scr = scratch
    g, mod, final_g = _layer_rows(mod_ref, g_ref, final_g_ref, layer)
    j = layer // 2
    ts = SEQ_TILE
    n_chunks = ts // CHUNK
    b = pl.program_id(0)
    s = pl.program_id(1)

    @pl.when((b == 0) & (s == 0))
    def _():
        tail_scr[...] = jnp.zeros(tail_scr.shape, _F32)

    row = lax.broadcasted_iota(jnp.int32, (CHUNK, CHUNK), 0)
    col = lax.broadcasted_iota(jnp.int32, (CHUNK, CHUNK), 1)
    causal = row >= col
    base_b = 4 * E_A
    heads_per_block = COL_BLOCK // HEAD_DIM
    n_blocks = E_B // COL_BLOCK
    bias_cols = []
    for head in range(H_B):
        b_row = jnp.broadcast_to(sgb_ref[head:head + 1, :], (CHUNK, CHUNK))
        bias_cols.append(jnp.concatenate([b_row.T] * n_chunks, axis=1))

    def mix_tile(ti, rows):
        def proj(seg_start, c0, width=COL_BLOCK):
            cols = slice(seg_start + c0, seg_start + c0 + width)
            return _dot(h_scr[rows], _weight(w_in_ref, 0, D_MODEL, cols))

        def a_block(c0):
            cols = slice(c0, c0 + COL_BLOCK)
            a_h = proj(0 * E_A, c0)
            a_c = proj(2 * E_A, c0)
            a_b = proj(1 * E_A, c0)
            a_z = proj(3 * E_A, c0)
            ch = a_c * a_h
            halo = tail_scr[:, cols]
            if ti == 0:
                halo = jnp.where(s > 0, halo, 0.0)
            tail_scr[:, cols] = ch[ts - CONV_HALO:]
            ext = jnp.concatenate([halo, ch], axis=0)
            w = conv_ref[:, cols]
            conv = (_shift_rows(ext, 2, CONV_HALO) * w[0:1]
                    + _shift_rows(ext, 1, CONV_HALO) * w[1:2] + ch * w[2:3])
            y_scr[rows, cols] = (a_b * conv * _silu(a_z)).astype(_BF16)

        def ln_blocks(c0):
            b_v = proj(base_b + 1 * E_B, c0, 2 * COL_BLOCK)
            rhs = []
            for hh in range(2 * heads_per_block):
                glanes = slice(c0 + hh * HEAD_DIM, c0 + (hh + 1) * HEAD_DIM)
                v = b_v[:, hh * HEAD_DIM:(hh + 1) * HEAD_DIM]
                dev = v - jnp.mean(v, axis=-1, keepdims=True)
                var = jnp.mean(dev * dev, axis=-1, keepdims=True)
                vn = (dev * lax.rsqrt(var + EPS) * lng_ref[j:j + 1, glanes]
                      + lnb_ref[j:j + 1, glanes])
                vn = vn.astype(_BF16)
                rhs.append(jnp.concatenate(
                    [vn[n * CHUNK:(n + 1) * CHUNK] for n in range(n_chunks)], axis=1))
            return [rhs[:heads_per_block], rhs[heads_per_block:]]

        def sgu_block(c0, rhs):
            mixed_heads = []
            for hh in range(heads_per_block):
                head = c0 // HEAD_DIM + hh
                w_h = jnp.where(causal, sgw_ref[head], 0.0).astype(_BF16)
                mixed = _dot(w_h, rhs[hh]) + bias_cols[head]
                mixed_heads.append(jnp.concatenate(
                    [mixed[:, n * CHUNK:(n + 1) * CHUNK] for n in range(n_chunks)], axis=0))
            mixed = jnp.concatenate(mixed_heads, axis=1)
            b_u = proj(base_b + 0 * E_B, c0)
            b_z = proj(base_b + 2 * E_B, c0)
            y_scr[rows, E_A + c0:E_A + c0 + COL_BLOCK] = (
                b_u * mixed * _silu(b_z)).astype(_BF16)

        rhs = []

        def head():
            h_scr[rows] = _modulated_input(x_ref[rows], g, mod).astype(_BF16)
            rhs.extend(ln_blocks(0))

        def body():
            for k in range(n_blocks):
                a_block(k * COL_BLOCK)
                if k % 2 == 0 and k + 2 < n_blocks:
                    rhs.extend(ln_blocks((k + 2) * COL_BLOCK))
                sgu_block(k * COL_BLOCK, rhs[k])

        def finish():
            out = None
            for k0 in range(0, E_INNER, COL_BLOCK):
                part = _dot(y_scr[rows, k0:k0 + COL_BLOCK],
                            _weight(w_out_ref, k0, k0 + COL_BLOCK, slice(None)))
                out = part if out is None else out + part
            o_ref[rows] = _residual_out(x_ref[rows], mod, out, final_g)

        return head, body, finish

    _cast_next_weights(cast_src, cast_dst)
    _run_tiles([mix_tile(ti, rows) for ti, rows in enumerate(_tiles(x_ref))])


def _fold_kernel(wp_ref, pw_ref, ps_ref, *rest):
    n_cast = (len(rest) - 1) // 2
    cast_src, wmix_ref, cast_dst = rest[:n_cast], rest[n_cast], rest[n_cast + 1:]

    def split(v):
        hi = v.astype(_BF16)
        return hi, (v - hi.astype(_F32)).astype(_BF16)

    a_hi, a_lo = split(wp_ref[...])
    b_hi, b_lo = split(pw_ref[...])
    folded = _dot(a_hi, b_hi) + (_dot(a_hi, b_lo) + _dot(a_lo, b_hi))
    scale = ps_ref[pl.ds(pl.program_id(0), 1), :]
    wmix_ref[...] = _pack_weight(folded * scale)
    _cast_next_weights(cast_src, cast_dst)


def _fold_pool_weights(c_w_in, c_pool_w, c_pool_scale, cast_jobs):
    n_odd = c_w_in.shape[0]
    n_groups = len(POOL_WINDOWS)
    c_in, c_args, c_out, c_shape = _cast_job_specs(
        cast_jobs, n_odd * n_groups, lambda j, g: j * n_groups + g)
    return pl.pallas_call(
        _fold_kernel,
        grid=(n_odd, n_groups),
        in_specs=[
            pl.BlockSpec((None, D_MODEL, G_C), lambda j, g: (j, 0, g)),
            pl.BlockSpec((None, None, G_C, G_C), lambda j, g: (j, g, 0, 0)),
            pl.BlockSpec((n_odd, G_C), lambda j, g: (0, g)),
        ] + c_in,
        out_specs=[pl.BlockSpec((None, D_MODEL // 2, G_C), lambda j, g: (j, 0, g))] + c_out,
        out_shape=[jax.ShapeDtypeStruct((n_odd, D_MODEL // 2, E_C), jnp.uint32)] + c_shape,
        compiler_params=pltpu.CompilerParams(
            dimension_semantics=("arbitrary", "arbitrary"),
            vmem_limit_bytes=VMEM_LIMIT_BYTES),
        name="fold_pool_weights",
    )(c_w_in, c_pool_w, c_pool_scale, *c_args)


def _odd_kernel(x_ref, mod_ref, g_ref, w_mix_ref, w_z_ref, w_out_ref, *rest, layer,
                final_norm, n_cast):
    final_g_ref, cast_src, o_ref, cast_dst, scratch = _split_refs(rest, final_norm, n_cast)
    h_scr, y_scr, tail_scr, lh_scr = scratch
    g, mod, final_g = _layer_rows(mod_ref, g_ref, final_g_ref, layer)
    ts = SEQ_TILE
    b = pl.program_id(0)
    s = pl.program_id(1)
    n_groups = len(POOL_WINDOWS)
    tiles = _tiles(x_ref)

    @pl.when((b == 0) & (s == 0))
    def _():
        tail_scr[...] = jnp.zeros(tail_scr.shape, _F32)

    def prepare(ti, rows):
        h = _modulated_input(x_ref[rows], g, mod)
        h_scr[rows] = h.astype(_BF16)
        halo = tail_scr[...]
        if ti == 0:
            halo = jnp.where(s > 0, halo, 0.0)
        tail_scr[...] = h[ts - POOL_HALO:]
        pos = ((s * len(tiles) + ti) * ts
               + lax.broadcasted_iota(jnp.int32, (ts, HEAD_DIM), 0))
        acc = jnp.concatenate([halo, h], axis=0)
        k = 1
        for gi, win in enumerate(POOL_WINDOWS):
            while k < win:
                acc = acc + pltpu.roll(acc, k, axis=0)
                k *= 2
            cnt = jnp.minimum(pos + 1, win).astype(_F32)
            inv_cnt = jnp.concatenate([1.0 / cnt] * (D_MODEL // HEAD_DIM), axis=1)
            lh_scr[gi, rows] = (acc[POOL_HALO:] * inv_cnt - h).astype(_BF16)

    def mix_tile(ti, rows):
        def gate_group(gi):
            gcols = slice(gi * G_C, (gi + 1) * G_C)
            return _silu(_dot(h_scr[rows], _weight(w_z_ref, 0, D_MODEL, gcols)))

        def mix_group(gi, gate):
            gcols = slice(gi * G_C, (gi + 1) * G_C)
            mixed = _dot(lh_scr[gi, rows], _weight(w_mix_ref, 0, D_MODEL, gcols))
            y_scr[rows, gcols] = (mixed * gate).astype(_BF16)

        lead = min(2, n_groups)
        gates = []

        def head():
            prepare(ti, rows)
            gates.extend(gate_group(gi) for gi in range(lead))
            mix_group(0, gates[0])

        def body():
            for gi in range(1, n_groups):
                if gi - 1 + lead < n_groups:
                    gates.append(gate_group(gi - 1 + lead))
                mix_group(gi, gates[gi])

        def finish():
            out = None
            for k0 in range(0, E_C, COL_BLOCK):
                part = _dot(y_scr[rows, k0:k0 + COL_BLOCK],
                            _weight(w_out_ref, k0, k0 + COL_BLOCK, slice(None)))
                out = part if out is None else out + part
            o_ref[rows] = _residual_out(x_ref[rows], mod, out, final_g)

        return head, body, finish

    _cast_next_weights(cast_src, cast_dst)
    _run_tiles([mix_tile(ti, rows) for ti, rows in enumerate(tiles)])


def _cast_job_specs(cast_jobs, n_steps, step_of):
    in_specs, args, out_specs, out_shape = [], [], [], []
    for src, j, col_block, n_cols in cast_jobs:
        n_rows = src.shape[1]
        rows = n_rows // n_steps
        assert rows * n_steps == n_rows and rows % 16 == 0
        in_specs.append(pl.BlockSpec(
            (None, rows, n_cols),
            lambda *g, j=j, cb=col_block: (j, step_of(*g), cb)))
        args.append(src)
        out_specs.append(pl.BlockSpec((rows // 2, n_cols), lambda *g: (step_of(*g), 0)))
        out_shape.append(jax.ShapeDtypeStruct((n_rows // 2, n_cols), jnp.uint32))
    return in_specs, args, out_specs, out_shape


def _layer_call(kernel_fn, name, layer, x, mod, norm_g, weights, weight_specs,
                scratch_shapes, final_g, cast_jobs):
    batch, seq, _ = x.shape
    n_seq = seq // SEQ_BLOCK
    n_steps = batch * n_seq
    x_spec = pl.BlockSpec((None, SEQ_BLOCK, D_MODEL), lambda b, s: (b, s, 0))
    in_specs = [
        x_spec,
        _resident((None, 3, batch, D_MODEL), lambda b, s: (layer, 0, 0, 0)),
        _resident((DEPTH, D_MODEL), lambda b, s: (0, 0)),
    ] + weight_specs
    args = [x, mod, norm_g] + weights
    if final_g is not None:
        in_specs.append(_resident((D_MODEL,), lambda b, s: (0,)))
        args.append(final_g)
    c_in, c_args, c_out, c_shape = _cast_job_specs(
        cast_jobs, n_steps, lambda b, s: b * n_seq + s)
    in_specs += c_in
    args += c_args
    out_specs = [x_spec] + c_out
    out_shape = [jax.ShapeDtypeStruct(x.shape, x.dtype)] + c_shape
    return pl.pallas_call(
        functools.partial(kernel_fn, layer=layer, final_norm=final_g is not None,
                          n_cast=len(cast_jobs)),
        grid=(batch, n_seq),
        in_specs=in_specs,
        out_specs=out_specs,
        out_shape=out_shape,
        scratch_shapes=[
            pltpu.VMEM((SEQ_BLOCK, D_MODEL), _BF16),
            pltpu.VMEM((SEQ_BLOCK, E_INNER), _BF16),
        ] + scratch_shapes,
        compiler_params=pltpu.CompilerParams(
            dimension_semantics=("arbitrary", "arbitrary"),
            vmem_limit_bytes=VMEM_LIMIT_BYTES),
        name=name,
    )(*args)


def _even_layer(layer, x, mod, norm_g, w_in, conv_w, ln_g, ln_b, sgu_w, sgu_b, w_out,
                final_g, cast_jobs):
    j = layer // 2
    n_even = conv_w.shape[0]
    const2 = lambda b, s: (0, 0)
    sel2 = lambda b, s: (j, 0, 0)
    sel3 = lambda b, s: (j, 0, 0, 0)
    weights = [w_in, conv_w, ln_g, ln_b, sgu_w, sgu_b, w_out]
    specs = [
        _resident((D_MODEL // 2, AB_IN), const2),
        _resident((None, CONV_WIDTH, E_A), sel2),
        _resident((n_even, E_B), const2),
        _resident((n_even, E_B), const2),
        _resident((None, H_B, CHUNK, CHUNK), sel3),
        _resident((None, H_B, CHUNK), sel2),
        _resident((E_INNER // 2, D_MODEL), const2),
    ]
    scratch = [pltpu.VMEM((CONV_HALO, E_A), _F32)]
    return _layer_call(_even_kernel, "even_layer", layer, x, mod, norm_g, weights, specs,
                       scratch, final_g, cast_jobs)


def _odd_layer(layer, x, mod, norm_g, w_mix, w_z, w_out, final_g, cast_jobs):
    j = layer // 2
    const2 = lambda b, s: (0, 0)
    weights = [w_mix, w_z, w_out]
    specs = [
        _resident((None, D_MODEL // 2, E_C), lambda b, s: (j, 0, 0)),
        _resident((D_MODEL // 2, E_C), const2),
        _resident((E_C // 2, D_MODEL), const2),
    ]
    scratch = [
        pltpu.VMEM((POOL_HALO, D_MODEL), _F32),
        pltpu.VMEM((len(POOL_WINDOWS), SEQ_BLOCK, D_MODEL), _BF16),
    ]
    return _layer_call(_odd_kernel, "odd_layer", layer, x, mod, norm_g, weights, specs,
                       scratch, final_g, cast_jobs)


def kernel(x, c, norm_g, ada_w, ada_b, ab_w_in, ab_conv_w, ab_ln_g, ab_ln_b, ab_sgu_w,
           ab_sgu_b, ab_w_out, c_w_in, c_pool_w, c_pool_scale, c_w_out, final_g):
    batch = x.shape[0]
    assert x.shape[1] % SEQ_BLOCK == 0 and SEQ_BLOCK % SEQ_TILE == 0
    assert SEQ_TILE % CHUNK == 0
    mod = _modulation(c, ada_w, ada_b)
    c_w_mix, *big_w = _fold_pool_weights(
        c_w_in, c_pool_w, c_pool_scale,
        [(ab_w_in, 0, 0, AB_IN), (ab_w_out, 0, 0, D_MODEL)])
    for i in range(DEPTH):
        fg = final_g if i == DEPTH - 1 else None
        nxt = (i + 1) // 2
        if i + 1 == DEPTH:
            cast_jobs = []
        elif i % 2 == 0:
            cast_jobs = [(c_w_in, nxt, 1, E_C), (c_w_out, nxt, 0, D_MODEL)]
        else:
            cast_jobs = [(ab_w_in, nxt, 0, AB_IN), (ab_w_out, nxt, 0, D_MODEL)]
        if i % 2 == 0:
            x, *big_w = _even_layer(i, x, mod, norm_g, big_w[0], ab_conv_w, ab_ln_g,
                                    ab_ln_b, ab_sgu_w, ab_sgu_b, big_w[1], fg, cast_jobs)
        else:
            x, *big_w = _odd_layer(i, x, mod, norm_g, c_w_mix, big_w[0], big_w[1], fg,
                                   cast_jobs)
    return x
```

```python
import functools

import jax
import jax.numpy as jnp
from jax import lax
from jax.experimental import pallas as pl
from jax.experimental.pallas import tpu as pltpu

D_MODEL = 1024
DEPTH = 4
E_INNER = 2 * D_MODEL
HEAD_DIM = 128
E_A = E_INNER // 2
E_B = E_INNER - E_A
H_B = E_B // HEAD_DIM
CONV_WIDTH = 3
CHUNK = 128
AB_IN = 4 * E_A + 3 * E_B
E_C = E_INNER
POOL_WINDOWS = (2, 4, 8, 16)
G_C = E_C // len(POOL_WINDOWS)
EPS = 1e-6

SEQ_BLOCK = 1024
SEQ_TILE = 512
COL_BLOCK = 256
CONV_HALO = 8
POOL_HALO = 16
V7X_VMEM_BYTES = 64 * 1024 * 1024
VMEM_LIMIT_BYTES = V7X_VMEM_BYTES - 8 * 1024 * 1024

assert all(w == 2 ** (i + 1) for i, w in enumerate(POOL_WINDOWS))
assert POOL_WINDOWS[-1] <= POOL_HALO and CONV_WIDTH - 1 <= CONV_HALO

_F32 = jnp.float32
_BF16 = jnp.bfloat16


_dot = functools.partial(jnp.dot, preferred_element_type=_F32)


def _silu(v):
    return v * jax.nn.sigmoid(v)


def _rmsnorm(x, g):
    return x * lax.rsqrt(jnp.mean(x * x, axis=-1, keepdims=True) + EPS) * g


def _resident(block_shape, index_map):
    return pl.BlockSpec(block_shape, index_map, pipeline_mode=pl.Buffered(1))


def _mod_kernel(c_ref, w_ref, b_ref, o_ref):
    c_act = _silu(c_ref[...]).astype(_BF16)
    bias = b_ref[pl.ds(pl.program_id(0), 1), :]
    o_ref[...] = _dot(c_act, w_ref[...].astype(_BF16)) + bias


def _modulation(c, ada_w, ada_b):
    batch = c.shape[0]
    return pl.pallas_call(
        _mod_kernel,
        grid=(DEPTH, 3),
        in_specs=[
            pl.BlockSpec((batch, D_MODEL), lambda i, j: (0, 0)),
            pl.BlockSpec((None, D_MODEL, D_MODEL), lambda i, j: (i, 0, j)),
            pl.BlockSpec((DEPTH, D_MODEL), lambda i, j: (0, j)),
        ],
        out_specs=pl.BlockSpec((None, None, batch, D_MODEL), lambda i, j: (i, j, 0, 0)),
        out_shape=jax.ShapeDtypeStruct((DEPTH, 3, batch, D_MODEL), _F32),
        compiler_params=pltpu.CompilerParams(
            dimension_semantics=("arbitrary", "arbitrary")),
        name="adaln_modulation",
    )(c, ada_w, ada_b)


def _layer_rows(mod_ref, g_ref, final_g_ref, layer):
    b = pl.program_id(0)
    mod = [mod_ref[k, pl.ds(b, 1), :] for k in range(3)]
    final_g = None if final_g_ref is None else final_g_ref[...].reshape(1, D_MODEL)
    return g_ref[layer:layer + 1, :], mod, final_g


def _modulated_input(x, g, mod):
    return _rmsnorm(x, g) * (1.0 + mod[1]) + mod[0]


def _project_out(x_ref, y_scr, w_out_ref, rows, mod, final_g, o_ref):
    out = None
    for k0 in range(0, y_scr.shape[1], COL_BLOCK):
        part = _dot(y_scr[rows, k0:k0 + COL_BLOCK],
                    _weight(w_out_ref, k0, k0 + COL_BLOCK, slice(None)))
        out = part if out is None else out + part
    res = x_ref[rows] + mod[2] * out
    if final_g is not None:
        res = _rmsnorm(res, final_g)
    o_ref[rows] = res


def _split_refs(rest, final_norm, n_cast):
    rest = list(rest)
    final_g_ref = rest.pop(0) if final_norm else None
    cast_src = [rest.pop(0) for _ in range(n_cast)]
    o_ref = rest.pop(0)
    cast_dst = [rest.pop(0) for _ in range(n_cast)]
    return final_g_ref, cast_src, o_ref, cast_dst, rest


def _pack_weight(w):
    return pltpu.bitcast(w.astype(_BF16), jnp.uint32)


def _weight(w_ref, k0, k1, cols):
    return pltpu.bitcast(w_ref[k0 // 2:k1 // 2, cols], _BF16)


def _cast_next_weights(cast_src, cast_dst):
    for src, dst in zip(cast_src, cast_dst):
        dst[...] = _pack_weight(src[...])


def _shift_rows(ext, k, halo):
    return pltpu.roll(ext, k, axis=0)[halo:]


def _tiles(x_ref):
    n_tiles = x_ref.shape[0] // SEQ_TILE
    return [slice(i * SEQ_TILE, (i + 1) * SEQ_TILE) for i in range(n_tiles)]


def _run_tiles(stages):
    stages[0][0]()
    for ti, (_, body, finish) in enumerate(stages):
        body()
        if ti + 1 < len(stages):
            stages[ti + 1][0]()
        finish()


def _even_kernel(x_ref, mod_ref, g_ref, w_in_ref, conv_ref, lng_ref, lnb_ref,
                 sgw_ref, sgb_ref, w_out_ref, *rest, layer, final_norm, n_cast):
    final_g_ref, cast_src, o_ref, cast_dst, scratch = _split_refs(rest, final_norm, n_cast)
    h_scr, y_scr, tail_scr = scratch
    g, mod, final_g = _layer_rows(mod_ref, g_ref, final_g_ref, layer)
    j = layer // 2
    ts = SEQ_TILE
    n_chunks = ts // CHUNK
    b = pl.program_id(0)
    s = pl.program_id(1)

    @pl.when((b == 0) & (s == 0))
    def _():
        tail_scr[...] = jnp.zeros(tail_scr.shape, _F32)

    row = lax.broadcasted_iota(jnp.int32, (CHUNK, CHUNK), 0)
    col = lax.broadcasted_iota(jnp.int32, (CHUNK, CHUNK), 1)
    causal = row >= col
    base_b = 4 * E_A
    heads_per_block = COL_BLOCK // HEAD_DIM
    n_blocks = E_B // COL_BLOCK
    bias_cols = []
    for head in range(H_B):
        b_row = jnp.broadcast_to(sgb_ref[head:head + 1, :], (CHUNK, CHUNK))
        bias_cols.append(jnp.concatenate([b_row.T] * n_chunks, axis=1))

    def mix_tile(ti, rows):
        def proj(seg_start, c0, width=COL_BLOCK):
            cols = slice(seg_start + c0, seg_start + c0 + width)
            return _dot(h_scr[rows], _weight(w_in_ref, 0, D_MODEL, cols))

        def a_block(c0):
            cols = slice(c0, c0 + COL_BLOCK)
            a_h = proj(0 * E_A, c0)
            a_c = proj(2 * E_A, c0)
            a_b = proj(1 * E_A, c0)
            a_z = proj(3 * E_A, c0)
            ch = a_c * a_h
            halo = tail_scr[:, cols]
            if ti == 0:
                halo = jnp.where(s > 0, halo, 0.0)
            tail_scr[:, cols] = ch[ts - CONV_HALO:]
            ext = jnp.concatenate([halo, ch], axis=0)
            w = conv_ref[:, cols]
            conv = (_shift_rows(ext, 2, CONV_HALO) * w[0:1]
                    + _shift_rows(ext, 1, CONV_HALO) * w[1:2] + ch * w[2:3])
            y_scr[rows, cols] = (a_b * conv * _silu(a_z)).astype(_BF16)

        def ln_blocks(c0):
            b_v = proj(base_b + 1 * E_B, c0, 2 * COL_BLOCK)
            rhs = []
            for hh in range(2 * heads_per_block):
                glanes = slice(c0 + hh * HEAD_DIM, c0 + (hh + 1) * HEAD_DIM)
                v = b_v[:, hh * HEAD_DIM:(hh + 1) * HEAD_DIM]
                dev = v - jnp.mean(v, axis=-1, keepdims=True)
                var = jnp.mean(dev * dev, axis=-1, keepdims=True)
                vn = (dev * lax.rsqrt(var + EPS) * lng_ref[j:j + 1, glanes]
                      + lnb_ref[j:j + 1, glanes])
                vn = vn.astype(_BF16)
                rhs.append(jnp.concatenate(
                    [vn[n * CHUNK:(n + 1) * CHUNK] for n in range(n_chunks)], axis=1))
            return [rhs[:heads_per_block], rhs[heads_per_block:]]

        def sgu_block(c0, rhs):
            mixed_heads = []
            for hh in range(heads_per_block):
                head = c0 // HEAD_DIM + hh
                w_h = jnp.where(causal, sgw_ref[head], 0.0).astype(_BF16)
                mixed = _dot(w_h, rhs[hh]) + bias_cols[head]
                mixed_heads.append(jnp.concatenate(
                    [mixed[:, n * CHUNK:(n + 1) * CHUNK] for n in range(n_chunks)], axis=0))
            mixed = jnp.concatenate(mixed_heads, axis=1)
            b_u = proj(base_b + 0 * E_B, c0)
            b_z = proj(base_b + 2 * E_B, c0)
            y_scr[rows, E_A + c0:E_A + c0 + COL_BLOCK] = (
                b_u * mixed * _silu(b_z)).astype(_BF16)

        rhs = []

        def head():
            h_scr[rows] = _modulated_input(x_ref[rows], g, mod).astype(_BF16)
            rhs.extend(ln_blocks(0))

        def body():
            for k in range(n_blocks):
                a_block(k * COL_BLOCK)
                if k % 2 == 0 and k + 2 < n_blocks:
                    rhs.extend(ln_blocks((k + 2) * COL_BLOCK))
                sgu_block(k * COL_BLOCK, rhs[k])

        def finish():
            _project_out(x_ref, y_scr, w_out_ref, rows, mod, final_g, o_ref)

        return head, body, finish

    _cast_next_weights(cast_src, cast_dst)
    _run_tiles([mix_tile(ti, rows) for ti, rows in enumerate(_tiles(x_ref))])


def _fold_kernel(wp_ref, pw_ref, ps_ref, *rest):
    n_cast = (len(rest) - 1) // 2
    cast_src, wmix_ref, cast_dst = rest[:n_cast], rest[n_cast], rest[n_cast + 1:]

    def split(v):
        hi = v.astype(_BF16)
        return hi, (v - hi.astype(_F32)).astype(_BF16)

    a_hi, a_lo = split(wp_ref[...])
    b_hi, b_lo = split(pw_ref[...])
    folded = _dot(a_hi, b_hi) + (_dot(a_hi, b_lo) + _dot(a_lo, b_hi))
    scale = ps_ref[pl.ds(pl.program_id(0), 1), :]
    wmix_ref[...] = _pack_weight(folded * scale)
    _cast_next_weights(cast_src, cast_dst)


def _fold_pool_weights(c_w_in, c_pool_w, c_pool_scale, cast_jobs):
    n_odd = c_w_in.shape[0]
    n_groups = len(POOL_WINDOWS)
    c_in, c_args, c_out, c_shape = _cast_job_specs(
        cast_jobs, n_odd * n_groups, lambda j, g: j * n_groups + g)
    return pl.pallas_call(
        _fold_kernel,
        grid=(n_odd, n_groups),
        in_specs=[
            pl.BlockSpec((None, D_MODEL, G_C), lambda j, g: (j, 0, g)),
            pl.BlockSpec((None, None, G_C, G_C), lambda j, g: (j, g, 0, 0)),
            pl.BlockSpec((n_odd, G_C), lambda j, g: (0, g)),
        ] + c_in,
        out_specs=[pl.BlockSpec((None, D_MODEL // 2, G_C), lambda j, g: (j, 0, g))] + c_out,
        out_shape=[jax.ShapeDtypeStruct((n_odd, D_MODEL // 2, E_C), jnp.uint32)] + c_shape,
        compiler_params=pltpu.CompilerParams(
            dimension_semantics=("arbitrary", "arbitrary"),
            vmem_limit_bytes=VMEM_LIMIT_BYTES),
        name="fold_pool_weights",
    )(c_w_in, c_pool_w, c_pool_scale, *c_args)


def _odd_kernel(x_ref, mod_ref, g_ref, w_mix_ref, w_z_ref, w_out_ref, *rest, layer,
                final_norm, n_cast):
    final_g_ref, cast_src, o_ref, cast_dst, scratch = _split_refs(rest, final_norm, n_cast)
    h_scr, y_scr, tail_scr, lh_scr = scratch
    g, mod, final_g = _layer_rows(mod_ref, g_ref, final_g_ref, layer)
    ts = SEQ_TILE
    b = pl.program_id(0)
    s = pl.program_id(1)
    n_groups = len(POOL_WINDOWS)
    tiles = _tiles(x_ref)

    @pl.when((b == 0) & (s == 0))
    def _():
        tail_scr[...] = jnp.zeros(tail_scr.shape, _F32)

    def window_sums(level, prev):
        sums, new_prev, k = [], [], 1
        for li, win in enumerate(POOL_WINDOWS):
            new_prev.append(level)
            before = level if prev is None else prev[li]
            ext = jnp.concatenate([before[POOL_HALO - 8:], level], axis=0)
            shifted = ext[8 - k:8 - k + POOL_HALO] if k == 8 else _shift_rows(ext, k, 8)
            level = level + shifted
            sums.append(level)
            k *= 2
        return sums, new_prev

    def prepare(ti, rows):
        halo = tail_scr[...]
        if ti == 0:
            halo = jnp.where(s > 0, halo, 0.0)
        _, prev = window_sums(halo, None)
        first_pos = (s * len(tiles) + ti) * ts
        for r0 in range(0, ts, POOL_HALO):
            grp = slice(rows.start + r0, rows.start + r0 + POOL_HALO)
            h = _modulated_input(x_ref[grp], g, mod)
            h_scr[grp] = h.astype(_BF16)
            sums, prev = window_sums(h, prev)
            pos = (first_pos + r0
                   + lax.broadcasted_iota(jnp.int32, (POOL_HALO, HEAD_DIM), 0))
            for gi, win in enumerate(POOL_WINDOWS):
                cnt = jnp.minimum(pos + 1, win).astype(_F32)
                inv_cnt = jnp.concatenate([1.0 / cnt] * (D_MODEL // HEAD_DIM), axis=1)
                lh_scr[gi, grp] = (sums[gi] * inv_cnt - h).astype(_BF16)
        tail_scr[...] = h

    def mix_tile(ti, rows):
        def gate_group(gi):
            gcols = slice(gi * G_C, (gi + 1) * G_C)
            return _silu(_dot(h_scr[rows], _weight(w_z_ref, 0, D_MODEL, gcols)))

        def mix_group(gi, gate):
            gcols = slice(gi * G_C, (gi + 1) * G_C)
            mixed = _dot(lh_scr[gi, rows], _weight(w_mix_ref, 0, D_MODEL, gcols))
            y_scr[rows, gcols] = (mixed * gate).astype(_BF16)

        lead = min(2, n_groups)
        gates = []

        def head():
            prepare(ti, rows)
            gates.extend(gate_group(gi) for gi in range(lead))
            mix_group(0, gates[0])

        def body():
            for gi in range(1, n_groups):
                if gi - 1 + lead < n_groups:
                    gates.append(gate_group(gi - 1 + lead))
                mix_group(gi, gates[gi])

        def finish():
            _project_out(x_ref, y_scr, w_out_ref, rows, mod, final_g, o_ref)

        return head, body, finish

    _cast_next_weights(cast_src, cast_dst)
    _run_tiles([mix_tile(ti, rows) for ti, rows in enumerate(tiles)])


def _cast_job_specs(cast_jobs, n_steps, step_of):
    in_specs, args, out_specs, out_shape = [], [], [], []
    for src, j, col_block, n_cols in cast_jobs:
        n_rows = src.shape[1]
        rows = n_rows // n_steps
        assert rows * n_steps == n_rows and rows % 16 == 0
        in_specs.append(pl.BlockSpec(
            (None, rows, n_cols),
            lambda *g, j=j, cb=col_block: (j, step_of(*g), cb)))
        args.append(src)
        out_specs.append(pl.BlockSpec((rows // 2, n_cols), lambda *g: (step_of(*g), 0)))
        out_shape.append(jax.ShapeDtypeStruct((n_rows // 2, n_cols), jnp.uint32))
    return in_specs, args, out_specs, out_shape


def _layer_call(kernel_fn, name, layer, x, mod, norm_g, weights, weight_specs,
                scratch_shapes, final_g, cast_jobs):
    batch, seq, _ = x.shape
    n_seq = seq // SEQ_BLOCK
    n_steps = batch * n_seq
    x_spec = pl.BlockSpec((None, SEQ_BLOCK, D_MODEL), lambda b, s: (b, s, 0))
    in_specs = [
        x_spec,
        _resident((None, 3, batch, D_MODEL), lambda b, s: (layer, 0, 0, 0)),
        _resident((DEPTH, D_MODEL), lambda b, s: (0, 0)),
    ] + weight_specs
    args = [x, mod, norm_g] + weights
    if final_g is not None:
        in_specs.append(_resident((D_MODEL,), lambda b, s: (0,)))
        args.append(final_g)
    c_in, c_args, c_out, c_shape = _cast_job_specs(
        cast_jobs, n_steps, lambda b, s: b * n_seq + s)
    in_specs += c_in
    args += c_args
    out_specs = [x_spec] + c_out
    out_shape = [jax.ShapeDtypeStruct(x.shape, x.dtype)] + c_shape
    return pl.pallas_call(
        functools.partial(kernel_fn, layer=layer, final_norm=final_g is not None,
                          n_cast=len(cast_jobs)),
        grid=(batch, n_seq),
        in_specs=in_specs,
        out_specs=out_specs,
        out_shape=out_shape,
        scratch_shapes=[
            pltpu.VMEM((SEQ_BLOCK, D_MODEL), _BF16),
            pltpu.VMEM((SEQ_BLOCK, E_INNER), _BF16),
        ] + scratch_shapes,
        compiler_params=pltpu.CompilerParams(
            dimension_semantics=("arbitrary", "arbitrary"),
            vmem_limit_bytes=VMEM_LIMIT_BYTES),
        name=name,
    )(*args)


def _even_layer(layer, x, mod, norm_g, w_in, conv_w, ln_g, ln_b, sgu_w, sgu_b, w_out,
                final_g, cast_jobs):
    j = layer // 2
    n_even = conv_w.shape[0]
    const2 = lambda b, s: (0, 0)
    sel2 = lambda b, s: (j, 0, 0)
    sel3 = lambda b, s: (j, 0, 0, 0)
    weights = [w_in, conv_w, ln_g, ln_b, sgu_w, sgu_b, w_out]
    specs = [
        _resident((D_MODEL // 2, AB_IN), const2),
        _resident((None, CONV_WIDTH, E_A), sel2),
        _resident((n_even, E_B), const2),
        _resident((n_even, E_B), const2),
        _resident((None, H_B, CHUNK, CHUNK), sel3),
        _resident((None, H_B, CHUNK), sel2),
        _resident((E_INNER // 2, D_MODEL), const2),
    ]
    scratch = [pltpu.VMEM((CONV_HALO, E_A), _F32)]
    return _layer_call(_even_kernel, "even_layer", layer, x, mod, norm_g, weights, specs,
                       scratch, final_g, cast_jobs)


def _odd_layer(layer, x, mod, norm_g, w_mix, w_z, w_out, final_g, cast_jobs):
    j = layer // 2
    const2 = lambda b, s: (0, 0)
    weights = [w_mix, w_z, w_out]
    specs = [
        _resident((None, D_MODEL // 2, E_C), lambda b, s: (j, 0, 0)),
        _resident((D_MODEL // 2, E_C), const2),
        _resident((E_C // 2, D_MODEL), const2),
    ]
    scratch = [
        pltpu.VMEM((POOL_HALO, D_MODEL), _F32),
        pltpu.VMEM((len(POOL_WINDOWS), SEQ_BLOCK, D_MODEL), _BF16),
    ]
    return _layer_call(_odd_kernel, "odd_layer", layer, x, mod, norm_g, weights, specs,
                       scratch, final_g, cast_jobs)


def kernel(x, c, norm_g, ada_w, ada_b, ab_w_in, ab_conv_w, ab_ln_g, ab_ln_b, ab_sgu_w,
           ab_sgu_b, ab_w_out, c_w_in, c_pool_w, c_pool_scale, c_w_out, final_g):
    batch = x.shape[0]
    assert x.shape[1] % SEQ_BLOCK == 0 and SEQ_BLOCK % SEQ_TILE == 0
    assert SEQ_TILE % CHUNK == 0
    mod = _modulation(c, ada_w, ada_b)
    c_w_mix, *big_w = _fold_pool_weights(
        c_w_in, c_pool_w, c_pool_scale,
        [(ab_w_in, 0, 0, AB_IN), (ab_w_out, 0, 0, D_MODEL)])
    for i in range(DEPTH):
        fg = final_g if i == DEPTH - 1 else None
        nxt = (i + 1) // 2
        if i + 1 == DEPTH:
            cast_jobs = []
        elif i % 2 == 0:
            cast_jobs = [(c_w_in, nxt, 1, E_C), (c_w_out, nxt, 0, D_MODEL)]
        else:
            cast_jobs = [(ab_w_in, nxt, 0, AB_IN), (ab_w_out, nxt, 0, D_MODEL)]
        if i % 2 == 0:
            x, *big_w = _even_layer(i, x, mod, norm_g, big_w[0], ab_conv_w, ab_ln_g,
                                    ab_ln_b, ab_sgu_w, ab_sgu_b, big_w[1], fg, cast_jobs)
        else:
            x, *big_w = _odd_layer(i, x, mod, norm_g, c_w_mix, big_w[0], big_w[1], fg,
                                   cast_jobs)
    return x
```

```python
import functools

import jax
import jax.numpy as jnp
from jax import lax
from jax.experimental import pallas as pl
from jax.experimental.pallas import tpu as pltpu

D_MODEL = 1024
DEPTH = 4
E_INNER = 2 * D_MODEL
HEAD_DIM = 128
E_A = E_INNER // 2
E_B = E_INNER - E_A
H_B = E_B // HEAD_DIM
CONV_WIDTH = 3
CHUNK = 128
AB_IN = 4 * E_A + 3 * E_B
E_C = E_INNER
POOL_WINDOWS = (2, 4, 8, 16)
G_C = E_C // len(POOL_WINDOWS)
EPS = 1e-6

SEQ_BLOCK = 1024
SEQ_TILE = 512
COL_BLOCK = 256
CONV_HALO = 8
POOL_HALO = 16
V7X_VMEM_BYTES = 64 * 1024 * 1024
VMEM_LIMIT_BYTES = V7X_VMEM_BYTES - 8 * 1024 * 1024

assert all(w == 2 ** (i + 1) for i, w in enumerate(POOL_WINDOWS))
assert POOL_WINDOWS[-1] <= POOL_HALO and CONV_WIDTH - 1 <= CONV_HALO

_F32 = jnp.float32
_BF16 = jnp.bfloat16


_dot = functools.partial(jnp.dot, preferred_element_type=_F32)


def _silu(v):
    return v * jax.nn.sigmoid(v)


def _rmsnorm(x, g):
    return x * lax.rsqrt(jnp.mean(x * x, axis=-1, keepdims=True) + EPS) * g


def _resident(block_shape, index_map):
    return pl.BlockSpec(block_shape, index_map, pipeline_mode=pl.Buffered(1))


def _mod_kernel(c_ref, w_ref, b_ref, o_ref):
    c_act = _silu(c_ref[...]).astype(_BF16)
    bias = b_ref[pl.ds(pl.program_id(0), 1), :]
    o_ref[...] = _dot(c_act, w_ref[...].astype(_BF16)) + bias


def _modulation(c, ada_w, ada_b):
    batch = c.shape[0]
    return pl.pallas_call(
        _mod_kernel,
        grid=(DEPTH, 3),
        in_specs=[
            pl.BlockSpec((batch, D_MODEL), lambda i, j: (0, 0)),
            pl.BlockSpec((None, D_MODEL, D_MODEL), lambda i, j: (i, 0, j)),
            pl.BlockSpec((DEPTH, D_MODEL), lambda i, j: (0, j)),
        ],
        out_specs=pl.BlockSpec((None, None, batch, D_MODEL), lambda i, j: (i, j, 0, 0)),
        out_shape=jax.ShapeDtypeStruct((DEPTH, 3, batch, D_MODEL), _F32),
        compiler_params=pltpu.CompilerParams(
            dimension_semantics=("arbitrary", "arbitrary")),
        name="adaln_modulation",
    )(c, ada_w, ada_b)


def _layer_rows(mod_ref, g_ref, final_g_ref, layer):
    b = pl.program_id(0)
    mod = [mod_ref[k, pl.ds(b, 1), :] for k in range(3)]
    final_g = None if final_g_ref is None else final_g_ref[...].reshape(1, D_MODEL)
    return g_ref[layer:layer + 1, :], mod, final_g


def _modulated_input(x, g, mod):
    return _rmsnorm(x, g) * (1.0 + mod[1]) + mod[0]


def _project_out(x_ref, y_scr, w_out_ref, rows, mod, final_g, o_ref):
    out = None
    for k0 in range(0, y_scr.shape[1], COL_BLOCK):
        part = _dot(y_scr[rows, k0:k0 + COL_BLOCK],
                    _weight(w_out_ref, k0, k0 + COL_BLOCK, slice(None)))
        out = part if out is None else out + part
    res = x_ref[rows] + mod[2] * out
    if final_g is not None:
        res = _rmsnorm(res, final_g)
    o_ref[rows] = res


def _split_refs(rest, final_norm, n_cast):
    rest = list(rest)
    final_g_ref = rest.pop(0) if final_norm else None
    cast_src = [rest.pop(0) for _ in range(n_cast)]
    o_ref = rest.pop(0)
    cast_dst = [rest.pop(0) for _ in range(n_cast)]
    return final_g_ref, cast_src, o_ref, cast_dst, rest


def _pack_weight(w):
    return pltpu.bitcast(w.astype(_BF16), jnp.uint32)


def _weight(w_ref, k0, k1, cols):
    return pltpu.bitcast(w_ref[k0 // 2:k1 // 2, cols], _BF16)


def _cast_next_weights(cast_src, cast_dst):
    for src, dst in zip(cast_src, cast_dst):
        dst[...] = _pack_weight(src[...])


def _shift_rows(ext, k, halo):
    return pltpu.roll(ext, k, axis=0)[halo:]


def _tiles(x_ref):
    n_tiles = x_ref.shape[0] // SEQ_TILE
    return [slice(i * SEQ_TILE, (i + 1) * SEQ_TILE) for i in range(n_tiles)]


def _run_tiles(stages):
    stages[0][0]()
    for ti, (_, body, finish) in enumerate(stages):
        body()
        if ti + 1 < len(stages):
            stages[ti + 1][0]()
        finish()


def _even_kernel(x_ref, mod_ref, g_ref, w_in_ref, conv_ref, lng_ref, lnb_ref,
                 sgw_ref, sgb_ref, w_out_ref, *rest, layer, final_norm, n_cast):
    final_g_ref, cast_src, o_ref, cast_dst, scratch = _split_refs(rest, final_norm, n_cast)
    h_scr, y_scr, tail_scr = scratch
    g, mod, final_g = _layer_rows(mod_ref, g_ref, final_g_ref, layer)
    j = layer // 2
    ts = SEQ_TILE
    n_chunks = ts // CHUNK
    b = pl.program_id(0)
    s = pl.program_id(1)

    @pl.when((b == 0) & (s == 0))
    def _():
        tail_scr[...] = jnp.zeros(tail_scr.shape, _F32)

    row = lax.broadcasted_iota(jnp.int32, (CHUNK, CHUNK), 0)
    col = lax.broadcasted_iota(jnp.int32, (CHUNK, CHUNK), 1)
    causal = row >= col
    base_b = 4 * E_A
    heads_per_block = COL_BLOCK // HEAD_DIM
    n_blocks = E_B // COL_BLOCK
    bias_cols = []
    for head in range(H_B):
        b_row = jnp.broadcast_to(sgb_ref[head:head + 1, :], (CHUNK, CHUNK))
        bias_cols.append(jnp.concatenate([b_row.T] * n_chunks, axis=1))

    def mix_tile(ti, rows):
        def proj(seg_start, c0, width=COL_BLOCK):
            cols = slice(seg_start + c0, seg_start + c0 + width)
            return _dot(h_scr[rows], _weight(w_in_ref, 0, D_MODEL, cols))

        def a_block(c0):
            cols = slice(c0, c0 + COL_BLOCK)
            a_h = proj(0 * E_A, c0)
            a_c = proj(2 * E_A, c0)
            a_b = proj(1 * E_A, c0)
            a_z = proj(3 * E_A, c0)
            ch = a_c * a_h
            halo = tail_scr[:, cols]
            if ti == 0:
                halo = jnp.where(s > 0, halo, 0.0)
            tail_scr[:, cols] = ch[ts - CONV_HALO:]
            ext = jnp.concatenate([halo, ch], axis=0)
            w = conv_ref[:, cols]
            conv = (_shift_rows(ext, 2, CONV_HALO) * w[0:1]
                    + _shift_rows(ext, 1, CONV_HALO) * w[1:2] + ch * w[2:3])
            y_scr[rows, cols] = (a_b * conv * _silu(a_z)).astype(_BF16)

        def ln_blocks(c0, n_blk):
            b_v = proj(base_b + 1 * E_B, c0, n_blk * COL_BLOCK)
            rhs = []
            for hh in range(n_blk * heads_per_block):
                glanes = slice(c0 + hh * HEAD_DIM, c0 + (hh + 1) * HEAD_DIM)
                v = b_v[:, hh * HEAD_DIM:(hh + 1) * HEAD_DIM]
                dev = v - jnp.mean(v, axis=-1, keepdims=True)
                var = jnp.mean(dev * dev, axis=-1, keepdims=True)
                vn = (dev * lax.rsqrt(var + EPS) * lng_ref[j:j + 1, glanes]
                      + lnb_ref[j:j + 1, glanes])
                vn = vn.astype(_BF16)
                rhs.append(jnp.concatenate(
                    [vn[n * CHUNK:(n + 1) * CHUNK] for n in range(n_chunks)], axis=1))
            return [rhs[i * heads_per_block:(i + 1) * heads_per_block] for i in range(n_blk)]

        def sgu_block(c0, rhs):
            mixed_heads = []
            for hh in range(heads_per_block):
                head = c0 // HEAD_DIM + hh
                w_h = jnp.where(causal, sgw_ref[head], 0.0).astype(_BF16)
                mixed = _dot(w_h, rhs[hh]) + bias_cols[head]
                mixed_heads.append(jnp.concatenate(
                    [mixed[:, n * CHUNK:(n + 1) * CHUNK] for n in range(n_chunks)], axis=0))
            mixed = jnp.concatenate(mixed_heads, axis=1)
            b_u = proj(base_b + 0 * E_B, c0)
            b_z = proj(base_b + 2 * E_B, c0)
            y_scr[rows, E_A + c0:E_A + c0 + COL_BLOCK] = (
                b_u * mixed * _silu(b_z)).astype(_BF16)

        rhs = []

        def head():
            h_scr[rows] = _modulated_input(x_ref[rows], g, mod).astype(_BF16)
            rhs.extend(ln_blocks(0, 2))

        def body():
            for k in range(n_blocks):
                a_block(k * COL_BLOCK)
                if k + 2 < n_blocks:
                    rhs.extend(ln_blocks((k + 2) * COL_BLOCK, 1))
                sgu_block(k * COL_BLOCK, rhs[k])

        def finish():
            _project_out(x_ref, y_scr, w_out_ref, rows, mod, final_g, o_ref)

        return head, body, finish

    _cast_next_weights(cast_src, cast_dst)
    _run_tiles([mix_tile(ti, rows) for ti, rows in enumerate(_tiles(x_ref))])


def _fold_kernel(wp_ref, pw_ref, ps_ref, *rest):
    n_cast = (len(rest) - 1) // 2
    cast_src, wmix_ref, cast_dst = rest[:n_cast], rest[n_cast], rest[n_cast + 1:]

    def split(v):
        hi = v.astype(_BF16)
        return hi, (v - hi.astype(_F32)).astype(_BF16)

    a_hi, a_lo = split(wp_ref[...])
    b_hi, b_lo = split(pw_ref[...])
    folded = _dot(a_hi, b_hi) + (_dot(a_hi, b_lo) + _dot(a_lo, b_hi))
    scale = ps_ref[pl.ds(pl.program_id(0), 1), :]
    wmix_ref[...] = _pack_weight(folded * scale)
    _cast_next_weights(cast_src, cast_dst)


def _fold_pool_weights(c_w_in, c_pool_w, c_pool_scale, cast_jobs):
    n_odd = c_w_in.shape[0]
    n_groups = len(POOL_WINDOWS)
    c_in, c_args, c_out, c_shape = _cast_job_specs(
        cast_jobs, n_odd * n_groups, lambda j, g: j * n_groups + g)
    return pl.pallas_call(
        _fold_kernel,
        grid=(n_odd, n_groups),
        in_specs=[
            pl.BlockSpec((None, D_MODEL, G_C), lambda j, g: (j, 0, g)),
            pl.BlockSpec((None, None, G_C, G_C), lambda j, g: (j, g, 0, 0)),
            pl.BlockSpec((n_odd, G_C), lambda j, g: (0, g)),
        ] + c_in,
        out_specs=[pl.BlockSpec((None, D_MODEL // 2, G_C), lambda j, g: (j, 0, g))] + c_out,
        out_shape=[jax.ShapeDtypeStruct((n_odd, D_MODEL // 2, E_C), jnp.uint32)] + c_shape,
        compiler_params=pltpu.CompilerParams(
            dimension_semantics=("arbitrary", "arbitrary"),
            vmem_limit_bytes=VMEM_LIMIT_BYTES),
        name="fold_pool_weights",
    )(c_w_in, c_pool_w, c_pool_scale, *c_args)


def _odd_kernel(x_ref, mod_ref, g_ref, w_mix_ref, w_z_ref, w_out_ref, *rest, layer,
                final_norm, n_cast):
    final_g_ref, cast_src, o_ref, cast_dst, scratch = _split_refs(rest, final_norm, n_cast)
    h_scr, y_scr, tail_scr, lh_scr = scratch
    g, mod, final_g = _layer_rows(mod_ref, g_ref, final_g_ref, layer)
    ts = SEQ_TILE
    b = pl.program_id(0)
    s = pl.program_id(1)
    n_groups = len(POOL_WINDOWS)
    tiles = _tiles(x_ref)

    @pl.when((b == 0) & (s == 0))
    def _():
        tail_scr[...] = jnp.zeros(tail_scr.shape, _F32)

    def window_sums(level, prev):
        sums, new_prev, k = [], [], 1
        for li, win in enumerate(POOL_WINDOWS):
            new_prev.append(level)
            before = level if prev is None else prev[li]
            ext = jnp.concatenate([before[POOL_HALO - 8:], level], axis=0)
            shifted = ext[8 - k:8 - k + POOL_HALO] if k == 8 else _shift_rows(ext, k, 8)
            level = level + shifted
            sums.append(level)
            k *= 2
        return sums, new_prev

    def prepare(ti, rows):
        halo = tail_scr[...]
        if ti == 0:
            halo = jnp.where(s > 0, halo, 0.0)
        _, prev = window_sums(halo, None)
        first_pos = (s * len(tiles) + ti) * ts
        for r0 in range(0, ts, POOL_HALO):
            grp = slice(rows.start + r0, rows.start + r0 + POOL_HALO)
            h = _modulated_input(x_ref[grp], g, mod)
            h_scr[grp] = h.astype(_BF16)
            sums, prev = window_sums(h, prev)
            pos = (first_pos + r0
                   + lax.broadcasted_iota(jnp.int32, (POOL_HALO, HEAD_DIM), 0))
            for gi, win in enumerate(POOL_WINDOWS):
                cnt = jnp.minimum(pos + 1, win).astype(_F32)
                inv_cnt = jnp.concatenate([1.0 / cnt] * (D_MODEL // HEAD_DIM), axis=1)
                lh_scr[gi, grp] = (sums[gi] * inv_cnt - h).astype(_BF16)
        tail_scr[...] = h

    def mix_tile(ti, rows):
        def gate_group(gi):
            gcols = slice(gi * G_C, (gi + 1) * G_C)
            return _silu(_dot(h_scr[rows], _weight(w_z_ref, 0, D_MODEL, gcols)))

        def mix_group(gi, gate):
            gcols = slice(gi * G_C, (gi + 1) * G_C)
            mixed = _dot(lh_scr[gi, rows], _weight(w_mix_ref, 0, D_MODEL, gcols))
            y_scr[rows, gcols] = (mixed * gate).astype(_BF16)

        lead = min(2, n_groups)
        gates = []

        def head():
            prepare(ti, rows)
            gates.extend(gate_group(gi) for gi in range(lead))
            mix_group(0, gates[0])

        def body():
            for gi in range(1, n_groups):
                if gi - 1 + lead < n_groups:
                    gates.append(gate_group(gi - 1 + lead))
                mix_group(gi, gates[gi])

        def finish():
            _project_out(x_ref, y_scr, w_out_ref, rows, mod, final_g, o_ref)

        return head, body, finish

    _cast_next_weights(cast_src, cast_dst)
    _run_tiles([mix_tile(ti, rows) for ti, rows in enumerate(tiles)])


def _cast_job_specs(cast_jobs, n_steps, step_of):
    in_specs, args, out_specs, out_shape = [], [], [], []
    for src, j, col_block, n_cols in cast_jobs:
        n_rows = src.shape[1]
        rows = n_rows // n_steps
        assert rows * n_steps == n_rows and rows % 16 == 0
        in_specs.append(pl.BlockSpec(
            (None, rows, n_cols),
            lambda *g, j=j, cb=col_block: (j, step_of(*g), cb)))
        args.append(src)
        out_specs.append(pl.BlockSpec((rows // 2, n_cols), lambda *g: (step_of(*g), 0)))
        out_shape.append(jax.ShapeDtypeStruct((n_rows // 2, n_cols), jnp.uint32))
    return in_specs, args, out_specs, out_shape


def _layer_call(kernel_fn, name, layer, x, mod, norm_g, weights, weight_specs,
                scratch_shapes, final_g, cast_jobs):
    batch, seq, _ = x.shape
    n_seq = seq // SEQ_BLOCK
    n_steps = batch * n_seq
    x_spec = pl.BlockSpec((None, SEQ_BLOCK, D_MODEL), lambda b, s: (b, s, 0))
    in_specs = [
        x_spec,
        _resident((None, 3, batch, D_MODEL), lambda b, s: (layer, 0, 0, 0)),
        _resident((DEPTH, D_MODEL), lambda b, s: (0, 0)),
    ] + weight_specs
    args = [x, mod, norm_g] + weights
    if final_g is not None:
        in_specs.append(_resident((D_MODEL,), lambda b, s: (0,)))
        args.append(final_g)
    c_in, c_args, c_out, c_shape = _cast_job_specs(
        cast_jobs, n_steps, lambda b, s: b * n_seq + s)
    in_specs += c_in
    args += c_args
    out_specs = [x_spec] + c_out
    out_shape = [jax.ShapeDtypeStruct(x.shape, x.dtype)] + c_shape
    return pl.pallas_call(
        functools.partial(kernel_fn, layer=layer, final_norm=final_g is not None,
                          n_cast=len(cast_jobs)),
        grid=(batch, n_seq),
        in_specs=in_specs,
        out_specs=out_specs,
        out_shape=out_shape,
        scratch_shapes=[
            pltpu.VMEM((SEQ_BLOCK, D_MODEL), _BF16),
            pltpu.VMEM((SEQ_BLOCK, E_INNER), _BF16),
        ] + scratch_shapes,
        compiler_params=pltpu.CompilerParams(
            dimension_semantics=("arbitrary", "arbitrary"),
            vmem_limit_bytes=VMEM_LIMIT_BYTES),
        name=name,
    )(*args)


def _even_layer(layer, x, mod, norm_g, w_in, conv_w, ln_g, ln_b, sgu_w, sgu_b, w_out,
                final_g, cast_jobs):
    j = layer // 2
    n_even = conv_w.shape[0]
    const2 = lambda b, s: (0, 0)
    sel2 = lambda b, s: (j, 0, 0)
    sel3 = lambda b, s: (j, 0, 0, 0)
    weights = [w_in, conv_w, ln_g, ln_b, sgu_w, sgu_b, w_out]
    specs = [
        _resident((D_MODEL // 2, AB_IN), const2),
        _resident((None, CONV_WIDTH, E_A), sel2),
        _resident((n_even, E_B), const2),
        _resident((n_even, E_B), const2),
        _resident((None, H_B, CHUNK, CHUNK), sel3),
        _resident((None, H_B, CHUNK), sel2),
        _resident((E_INNER // 2, D_MODEL), const2),
    ]
    scratch = [pltpu.VMEM((CONV_HALO, E_A), _F32)]
    return _layer_call(_even_kernel, "even_layer", layer, x, mod, norm_g, weights, specs,
                       scratch, final_g, cast_jobs)


def _odd_layer(layer, x, mod, norm_g, w_mix, w_z, w_out, final_g, cast_jobs):
    j = layer // 2
    const2 = lambda b, s: (0, 0)
    weights = [w_mix, w_z, w_out]
    specs = [
        _resident((None, D_MODEL // 2, E_C), lambda b, s: (j, 0, 0)),
        _resident((D_MODEL // 2, E_C), const2),
        _resident((E_C // 2, D_MODEL), const2),
    ]
    scratch = [
        pltpu.VMEM((POOL_HALO, D_MODEL), _F32),
        pltpu.VMEM((len(POOL_WINDOWS), SEQ_BLOCK, D_MODEL), _BF16),
    ]
    return _layer_call(_odd_kernel, "odd_layer", layer, x, mod, norm_g, weights, specs,
                       scratch, final_g, cast_jobs)


def kernel(x, c, norm_g, ada_w, ada_b, ab_w_in, ab_conv_w, ab_ln_g, ab_ln_b, ab_sgu_w,
           ab_sgu_b, ab_w_out, c_w_in, c_pool_w, c_pool_scale, c_w_out, final_g):
    batch = x.shape[0]
    assert x.shape[1] % SEQ_BLOCK == 0 and SEQ_BLOCK % SEQ_TILE == 0
    assert SEQ_TILE % CHUNK == 0
    mod = _modulation(c, ada_w, ada_b)
    c_w_mix, *big_w = _fold_pool_weights(
        c_w_in, c_pool_w, c_pool_scale,
        [(ab_w_in, 0, 0, AB_IN), (ab_w_out, 0, 0, D_MODEL)])
    for i in range(DEPTH):
        fg = final_g if i == DEPTH - 1 else None
        nxt = (i + 1) // 2
        if i + 1 == DEPTH:
            cast_jobs = []
        elif i % 2 == 0:
            cast_jobs = [(c_w_in, nxt, 1, E_C), (c_w_out, nxt, 0, D_MODEL)]
        else:
            cast_jobs = [(ab_w_in, nxt, 0, AB_IN), (ab_w_out, nxt, 0, D_MODEL)]
        if i % 2 == 0:
            x, *big_w = _even_layer(i, x, mod, norm_g, big_w[0], ab_conv_w, ab_ln_g,
                                    ab_ln_b, ab_sgu_w, ab_sgu_b, big_w[1], fg, cast_jobs)
        else:
            x, *big_w = _odd_layer(i, x, mod, norm_g, c_w_mix, big_w[0], big_w[1], fg,
                                   cast_jobs)
    return x
```

```python
import functools

import jax
import jax.numpy as jnp
from jax import lax
from jax.experimental import pallas as pl
from jax.experimental.pallas import tpu as pltpu

D_MODEL = 1024
DEPTH = 4
E_INNER = 2 * D_MODEL
HEAD_DIM = 128
E_A = E_INNER // 2
E_B = E_INNER - E_A
H_B = E_B // HEAD_DIM
CONV_WIDTH = 3
CHUNK = 128
AB_IN = 4 * E_A + 3 * E_B
E_C = E_INNER
POOL_WINDOWS = (2, 4, 8, 16)
G_C = E_C // len(POOL_WINDOWS)
EPS = 1e-6

SEQ_BLOCK = 1024
SEQ_TILE = 512
COL_BLOCK = 256
CONV_HALO = 8
POOL_HALO = 16
PREPARE_STEPS = 16
V7X_VMEM_BYTES = 64 * 1024 * 1024
VMEM_LIMIT_BYTES = V7X_VMEM_BYTES - 8 * 1024 * 1024

assert all(w == 2 ** (i + 1) for i, w in enumerate(POOL_WINDOWS))
assert POOL_WINDOWS[-1] <= POOL_HALO and CONV_WIDTH - 1 <= CONV_HALO

_F32 = jnp.float32
_BF16 = jnp.bfloat16


_dot = functools.partial(jnp.dot, preferred_element_type=_F32)


def _silu(v):
    return v * jax.nn.sigmoid(v)


def _rmsnorm(x, g):
    return x * lax.rsqrt(jnp.mean(x * x, axis=-1, keepdims=True) + EPS) * g


def _resident(block_shape, index_map):
    return pl.BlockSpec(block_shape, index_map, pipeline_mode=pl.Buffered(1))


def _layer_rows(mod_ref, g_ref, final_g_ref, layer):
    b = pl.program_id(0)
    mod = [mod_ref[k, pl.ds(b, 1), :] for k in range(3)]
    final_g = None if final_g_ref is None else final_g_ref[...].reshape(1, D_MODEL)
    return g_ref[layer:layer + 1, :], mod, final_g


def _modulated_input(x, g, mod):
    return _rmsnorm(x, g) * (1.0 + mod[1]) + mod[0]


def _project_out(x_ref, y_scr, w_out_ref, rows, mod, final_g, o_ref):
    out = None
    for k0 in range(0, y_scr.shape[1], COL_BLOCK):
        part = _dot(y_scr[rows, k0:k0 + COL_BLOCK],
                    _weight(w_out_ref, k0, k0 + COL_BLOCK, slice(None)))
        out = part if out is None else out + part
    res = x_ref[rows] + mod[2] * out
    if final_g is not None:
        res = _rmsnorm(res, final_g)
    o_ref[rows] = res


def _split_refs(rest, final_norm, n_cast):
    rest = list(rest)
    final_g_ref = rest.pop(0) if final_norm else None
    cast_src = [rest.pop(0) for _ in range(n_cast)]
    o_ref = rest.pop(0)
    cast_dst = [rest.pop(0) for _ in range(n_cast)]
    return final_g_ref, cast_src, o_ref, cast_dst, rest


def _pack_weight(w):
    return pltpu.bitcast(w.astype(_BF16), jnp.uint32)


def _weight(w_ref, k0, k1, cols):
    return pltpu.bitcast(w_ref[k0 // 2:k1 // 2, cols], _BF16)


def _cast_next_weights(cast_src, cast_dst):
    for src, dst in zip(cast_src, cast_dst):
        dst[...] = _pack_weight(src[...])


def _shift_rows(ext, k, halo):
    return pltpu.roll(ext, k, axis=0)[halo:]


def _tiles(x_ref):
    n_tiles = x_ref.shape[0] // SEQ_TILE
    return [slice(i * SEQ_TILE, (i + 1) * SEQ_TILE) for i in range(n_tiles)]


def _run_tiles(stages):
    stages[0][0]()
    for ti, (_, body, finish) in enumerate(stages):
        body()
        if ti + 1 < len(stages):
            stages[ti + 1][0]()
        finish()


def _even_kernel(x_ref, mod_ref, g_ref, w_in_ref, conv_ref, lng_ref, lnb_ref,
                 sgw_ref, sgb_ref, w_out_ref, *rest, layer, final_norm, n_cast):
    final_g_ref, cast_src, o_ref, cast_dst, scratch = _split_refs(rest, final_norm, n_cast)
    h_scr, y_scr, tail_scr = scratch
    g, mod, final_g = _layer_rows(mod_ref, g_ref, final_g_ref, layer)
    j = layer // 2
    ts = SEQ_TILE
    n_chunks = ts // CHUNK
    b = pl.program_id(0)
    s = pl.program_id(1)

    @pl.when((b == 0) & (s == 0))
    def _():
        tail_scr[...] = jnp.zeros(tail_scr.shape, _F32)

    row = lax.broadcasted_iota(jnp.int32, (CHUNK, CHUNK), 0)
    col = lax.broadcasted_iota(jnp.int32, (CHUNK, CHUNK), 1)
    causal = row >= col
    base_b = 4 * E_A
    heads_per_block = COL_BLOCK // HEAD_DIM
    n_blocks = E_B // COL_BLOCK
    bias_cols = []
    for head in range(H_B):
        b_row = jnp.broadcast_to(sgb_ref[head:head + 1, :], (CHUNK, CHUNK))
        bias_cols.append(jnp.concatenate([b_row.T] * n_chunks, axis=1))

    def mix_tile(ti, rows):
        def proj(seg_start, c0, width=COL_BLOCK):
            cols = slice(seg_start + c0, seg_start + c0 + width)
            return _dot(h_scr[rows], _weight(w_in_ref, 0, D_MODEL, cols))

        def a_block(c0):
            cols = slice(c0, c0 + COL_BLOCK)
            a_h = proj(0 * E_A, c0)
            a_c = proj(2 * E_A, c0)
            a_b = proj(1 * E_A, c0)
            a_z = proj(3 * E_A, c0)
            ch = a_c * a_h
            halo = tail_scr[:, cols]
            if ti == 0:
                halo = jnp.where(s > 0, halo, 0.0)
            tail_scr[:, cols] = ch[ts - CONV_HALO:]
            ext = jnp.concatenate([halo, ch], axis=0)
            w = conv_ref[:, cols]
            conv = (_shift_rows(ext, 2, CONV_HALO) * w[0:1]
                    + _shift_rows(ext, 1, CONV_HALO) * w[1:2] + ch * w[2:3])
            y_scr[rows, cols] = (a_b * conv * _silu(a_z)).astype(_BF16)

        def ln_blocks(c0, n_blk):
            b_v = proj(base_b + 1 * E_B, c0, n_blk * COL_BLOCK)
            rhs = []
            for hh in range(n_blk * heads_per_block):
                glanes = slice(c0 + hh * HEAD_DIM, c0 + (hh + 1) * HEAD_DIM)
                v = b_v[:, hh * HEAD_DIM:(hh + 1) * HEAD_DIM]
                dev = v - jnp.mean(v, axis=-1, keepdims=True)
                var = jnp.mean(dev * dev, axis=-1, keepdims=True)
                vn = (dev * lax.rsqrt(var + EPS) * lng_ref[j:j + 1, glanes]
                      + lnb_ref[j:j + 1, glanes])
                vn = vn.astype(_BF16)
                rhs.append(jnp.concatenate(
                    [vn[n * CHUNK:(n + 1) * CHUNK] for n in range(n_chunks)], axis=1))
            return [rhs[i * heads_per_block:(i + 1) * heads_per_block] for i in range(n_blk)]

        def sgu_block(c0, rhs):
            mixed_heads = []
            for hh in range(heads_per_block):
                head = c0 // HEAD_DIM + hh
                w_h = jnp.where(causal, sgw_ref[head], 0.0).astype(_BF16)
                mixed = _dot(w_h, rhs[hh]) + bias_cols[head]
                mixed_heads.append(jnp.concatenate(
                    [mixed[:, n * CHUNK:(n + 1) * CHUNK] for n in range(n_chunks)], axis=0))
            mixed = jnp.concatenate(mixed_heads, axis=1)
            b_u = proj(base_b + 0 * E_B, c0)
            b_z = proj(base_b + 2 * E_B, c0)
            y_scr[rows, E_A + c0:E_A + c0 + COL_BLOCK] = (
                b_u * mixed * _silu(b_z)).astype(_BF16)

        rhs = []

        def head():
            h_scr[rows] = _modulated_input(x_ref[rows], g, mod).astype(_BF16)
            rhs.extend(ln_blocks(0, 2))

        def body():
            for k in range(n_blocks):
                a_block(k * COL_BLOCK)
                if k + 2 < n_blocks:
                    rhs.extend(ln_blocks((k + 2) * COL_BLOCK, 1))
                sgu_block(k * COL_BLOCK, rhs[k])

        def finish():
            _project_out(x_ref, y_scr, w_out_ref, rows, mod, final_g, o_ref)

        return head, body, finish

    _cast_next_weights(cast_src, cast_dst)
    _run_tiles([mix_tile(ti, rows) for ti, rows in enumerate(_tiles(x_ref))])


def _prepare_kernel(c_ref, ada_w_ref, ada_b_ref, wp_ref, pw_ref, ps_ref, *rest, n_mod_units,
                    n_fold_units):
    n_cast = (len(rest) - 2) // 2
    cast_src, cast_dst = rest[:n_cast], rest[n_cast + 2:]
    mod_ref, wmix_ref = rest[n_cast:n_cast + 2]
    step = pl.program_id(0)

    @pl.when(step < n_mod_units)
    def _():
        c_act = _silu(c_ref[...]).astype(_BF16)
        bias = ada_b_ref[pl.ds(step // 3, 1), :]
        mod_ref[...] = _dot(c_act, ada_w_ref[...].astype(_BF16)) + bias

    @pl.when(step < n_fold_units)
    def _():
        def split(v):
            hi = v.astype(_BF16)
            return hi, (v - hi.astype(_F32)).astype(_BF16)

        a_hi, a_lo = split(wp_ref[...])
        b_hi, b_lo = split(pw_ref[...])
        folded = _dot(a_hi, b_hi) + (_dot(a_hi, b_lo) + _dot(a_lo, b_hi))
        scale = ps_ref[pl.ds(step // len(POOL_WINDOWS), 1), :]
        wmix_ref[...] = _pack_weight(folded * scale)

    _cast_next_weights(cast_src, cast_dst)


def _prepare_params(c, ada_w, ada_b, c_w_in, c_pool_w, c_pool_scale, cast_jobs):
    batch = c.shape[0]
    n_odd = c_w_in.shape[0]
    n_groups = len(POOL_WINDOWS)
    n_mod_units, n_fold_units = DEPTH * 3, n_odd * n_groups
    n_steps = PREPARE_STEPS
    assert n_steps >= max(n_mod_units, n_fold_units)

    def mod_unit(i):
        u = jnp.minimum(i, n_mod_units - 1)
        return u // 3, u % 3

    def fold_unit(i):
        u = jnp.minimum(i, n_fold_units - 1)
        return u // n_groups, u % n_groups

    c_in, c_args, c_out, c_shape = _cast_job_specs(cast_jobs, n_steps, lambda i: i)
    return pl.pallas_call(
        functools.partial(_prepare_kernel, n_mod_units=n_mod_units,
                          n_fold_units=n_fold_units),
        grid=(n_steps,),
        in_specs=[
            pl.BlockSpec((batch, D_MODEL), lambda i: (0, 0)),
            pl.BlockSpec((None, D_MODEL, D_MODEL),
                         lambda i: (mod_unit(i)[0], 0, mod_unit(i)[1])),
            pl.BlockSpec((DEPTH, D_MODEL), lambda i: (0, mod_unit(i)[1])),
            pl.BlockSpec((None, D_MODEL, G_C),
                         lambda i: (fold_unit(i)[0], 0, fold_unit(i)[1])),
            pl.BlockSpec((None, None, G_C, G_C), lambda i: (*fold_unit(i), 0, 0)),
            pl.BlockSpec((n_odd, G_C), lambda i: (0, fold_unit(i)[1])),
        ] + c_in,
        out_specs=[
            pl.BlockSpec((None, None, batch, D_MODEL), lambda i: (*mod_unit(i), 0, 0)),
            pl.BlockSpec((None, D_MODEL // 2, G_C),
                         lambda i: (fold_unit(i)[0], 0, fold_unit(i)[1])),
        ] + c_out,
        out_shape=[
            jax.ShapeDtypeStruct((DEPTH, 3, batch, D_MODEL), _F32),
            jax.ShapeDtypeStruct((n_odd, D_MODEL // 2, E_C), jnp.uint32),
        ] + c_shape,
        compiler_params=pltpu.CompilerParams(
            dimension_semantics=("arbitrary",),
            vmem_limit_bytes=VMEM_LIMIT_BYTES),
        name="prepare_params",
    )(c, ada_w, ada_b, c_w_in, c_pool_w, c_pool_scale, *c_args)


def _odd_kernel(x_ref, mod_ref, g_ref, w_mix_ref, w_z_ref, w_out_ref, *rest, layer,
                final_norm, n_cast):
    final_g_ref, cast_src, o_ref, cast_dst, scratch = _split_refs(rest, final_norm, n_cast)
    h_scr, y_scr, tail_scr, lh_scr = scratch
    g, mod, final_g = _layer_rows(mod_ref, g_ref, final_g_ref, layer)
    ts = SEQ_TILE
    b = pl.program_id(0)
    s = pl.program_id(1)
    n_groups = len(POOL_WINDOWS)
    tiles = _tiles(x_ref)

    @pl.when((b == 0) & (s == 0))
    def _():
        tail_scr[...] = jnp.zeros(tail_scr.shape, _F32)

    def window_sums(level, prev):
        sums, new_prev, k = [], [], 1
        for li, win in enumerate(POOL_WINDOWS):
            new_prev.append(level)
            before = level if prev is None else prev[li]
            ext = jnp.concatenate([before[POOL_HALO - 8:], level], axis=0)
            shifted = ext[8 - k:8 - k + POOL_HALO] if k == 8 else _shift_rows(ext, k, 8)
            level = level + shifted
            sums.append(level)
            k *= 2
        return sums, new_prev

    def prepare(ti, rows):
        halo = tail_scr[...]
        if ti == 0:
            halo = jnp.where(s > 0, halo, 0.0)
        _, prev = window_sums(halo, None)
        first_pos = (s * len(tiles) + ti) * ts
        for r0 in range(0, ts, POOL_HALO):
            grp = slice(rows.start + r0, rows.start + r0 + POOL_HALO)
            h = _modulated_input(x_ref[grp], g, mod)
            h_scr[grp] = h.astype(_BF16)
            sums, prev = window_sums(h, prev)
            pos = (first_pos + r0
                   + lax.broadcasted_iota(jnp.int32, (POOL_HALO, HEAD_DIM), 0))
            for gi, win in enumerate(POOL_WINDOWS):
                cnt = jnp.minimum(pos + 1, win).astype(_F32)
                inv_cnt = jnp.concatenate([1.0 / cnt] * (D_MODEL // HEAD_DIM), axis=1)
                lh_scr[gi, grp] = (sums[gi] * inv_cnt - h).astype(_BF16)
        tail_scr[...] = h

    def mix_tile(ti, rows):
        def gate_group(gi):
            gcols = slice(gi * G_C, (gi + 1) * G_C)
            return _silu(_dot(h_scr[rows], _weight(w_z_ref, 0, D_MODEL, gcols)))

        def mix_group(gi, gate):
            gcols = slice(gi * G_C, (gi + 1) * G_C)
            mixed = _dot(lh_scr[gi, rows], _weight(w_mix_ref, 0, D_MODEL, gcols))
            y_scr[rows, gcols] = (mixed * gate).astype(_BF16)

        lead = min(2, n_groups)
        gates = []

        def head():
            prepare(ti, rows)
            gates.extend(gate_group(gi) for gi in range(lead))
            mix_group(0, gates[0])

        def body():
            for gi in range(1, n_groups):
                if gi - 1 + lead < n_groups:
                    gates.append(gate_group(gi - 1 + lead))
                mix_group(gi, gates[gi])

        def finish():
            _project_out(x_ref, y_scr, w_out_ref, rows, mod, final_g, o_ref)

        return head, body, finish

    _cast_next_weights(cast_src, cast_dst)
    _run_tiles([mix_tile(ti, rows) for ti, rows in enumerate(tiles)])


def _cast_job_specs(cast_jobs, n_steps, step_of):
    in_specs, args, out_specs, out_shape = [], [], [], []
    for src, j, col_block, n_cols in cast_jobs:
        n_rows = src.shape[1]
        rows = n_rows // n_steps
        assert rows * n_steps == n_rows and rows % 16 == 0
        in_specs.append(pl.BlockSpec(
            (None, rows, n_cols),
            lambda *g, j=j, cb=col_block: (j, step_of(*g), cb)))
        args.append(src)
        out_specs.append(pl.BlockSpec((rows // 2, n_cols), lambda *g: (step_of(*g), 0)))
        out_shape.append(jax.ShapeDtypeStruct((n_rows // 2, n_cols), jnp.uint32))
    return in_specs, args, out_specs, out_shape


def _layer_call(kernel_fn, name, layer, x, mod, norm_g, weights, weight_specs,
                scratch_shapes, final_g, cast_jobs):
    batch, seq, _ = x.shape
    n_seq = seq // SEQ_BLOCK
    n_steps = batch * n_seq
    x_spec = pl.BlockSpec((None, SEQ_BLOCK, D_MODEL), lambda b, s: (b, s, 0))
    in_specs = [
        x_spec,
        _resident((None, 3, batch, D_MODEL), lambda b, s: (layer, 0, 0, 0)),
        _resident((DEPTH, D_MODEL), lambda b, s: (0, 0)),
    ] + weight_specs
    args = [x, mod, norm_g] + weights
    if final_g is not None:
        in_specs.append(_resident((D_MODEL,), lambda b, s: (0,)))
        args.append(final_g)
    c_in, c_args, c_out, c_shape = _cast_job_specs(
        cast_jobs, n_steps, lambda b, s: b * n_seq + s)
    in_specs += c_in
    args += c_args
    out_specs = [x_spec] + c_out
    out_shape = [jax.ShapeDtypeStruct(x.shape, x.dtype)] + c_shape
    return pl.pallas_call(
        functools.partial(kernel_fn, layer=layer, final_norm=final_g is not None,
                          n_cast=len(cast_jobs)),
        grid=(batch, n_seq),
        in_specs=in_specs,
        out_specs=out_specs,
        out_shape=out_shape,
        scratch_shapes=[
            pltpu.VMEM((SEQ_BLOCK, D_MODEL), _BF16),
            pltpu.VMEM((SEQ_BLOCK, E_INNER), _BF16),
        ] + scratch_shapes,
        compiler_params=pltpu.CompilerParams(
            dimension_semantics=("arbitrary", "arbitrary"),
            vmem_limit_bytes=VMEM_LIMIT_BYTES),
        name=name,
    )(*args)


def _even_layer(layer, x, mod, norm_g, w_in, conv_w, ln_g, ln_b, sgu_w, sgu_b, w_out,
                final_g, cast_jobs):
    j = layer // 2
    n_even = conv_w.shape[0]
    const2 = lambda b, s: (0, 0)
    sel2 = lambda b, s: (j, 0, 0)
    sel3 = lambda b, s: (j, 0, 0, 0)
    weights = [w_in, conv_w, ln_g, ln_b, sgu_w, sgu_b, w_out]
    specs = [
        _resident((D_MODEL // 2, AB_IN), const2),
        _resident((None, CONV_WIDTH, E_A), sel2),
        _resident((n_even, E_B), const2),
        _resident((n_even, E_B), const2),
        _resident((None, H_B, CHUNK, CHUNK), sel3),
        _resident((None, H_B, CHUNK), sel2),
        _resident((E_INNER // 2, D_MODEL), const2),
    ]
    scratch = [pltpu.VMEM((CONV_HALO, E_A), _F32)]
    return _layer_call(_even_kernel, "even_layer", layer, x, mod, norm_g, weights, specs,
                       scratch, final_g, cast_jobs)


def _odd_layer(layer, x, mod, norm_g, w_mix, w_z, w_out, final_g, cast_jobs):
    j = layer // 2
    const2 = lambda b, s: (0, 0)
    weights = [w_mix, w_z, w_out]
    specs = [
        _resident((None, D_MODEL // 2, E_C), lambda b, s: (j, 0, 0)),
        _resident((D_MODEL // 2, E_C), const2),
        _resident((E_C // 2, D_MODEL), const2),
    ]
    scratch = [
        pltpu.VMEM((POOL_HALO, D_MODEL), _F32),
        pltpu.VMEM((len(POOL_WINDOWS), SEQ_BLOCK, D_MODEL), _BF16),
    ]
    return _layer_call(_odd_kernel, "odd_layer", layer, x, mod, norm_g, weights, specs,
                       scratch, final_g, cast_jobs)


def kernel(x, c, norm_g, ada_w, ada_b, ab_w_in, ab_conv_w, ab_ln_g, ab_ln_b, ab_sgu_w,
           ab_sgu_b, ab_w_out, c_w_in, c_pool_w, c_pool_scale, c_w_out, final_g):
    batch = x.shape[0]
    assert x.shape[1] % SEQ_BLOCK == 0 and SEQ_BLOCK % SEQ_TILE == 0
    assert SEQ_TILE % CHUNK == 0
    mod, c_w_mix, *big_w = _prepare_params(
        c, ada_w, ada_b, c_w_in, c_pool_w, c_pool_scale,
        [(ab_w_in, 0, 0, AB_IN), (ab_w_out, 0, 0, D_MODEL)])
    for i in range(DEPTH):
        fg = final_g if i == DEPTH - 1 else None
        nxt = (i + 1) // 2
        if i + 1 == DEPTH:
            cast_jobs = []
        elif i % 2 == 0:
            cast_jobs = [(c_w_in, nxt, 1, E_C), (c_w_out, nxt, 0, D_MODEL)]
        else:
            cast_jobs = [(ab_w_in, nxt, 0, AB_IN), (ab_w_out, nxt, 0, D_MODEL)]
        if i % 2 == 0:
            x, *big_w = _even_layer(i, x, mod, norm_g, big_w[0], ab_conv_w, ab_ln_g,
                                    ab_ln_b, ab_sgu_w, ab_sgu_b, big_w[1], fg, cast_jobs)
        else:
            x, *big_w = _odd_layer(i, x, mod, norm_g, c_w_mix, big_w[0], big_w[1], fg,
                                   cast_jobs)
    return x
```

```python
import functools

import jax
import jax.numpy as jnp
from jax import lax
from jax.experimental import pallas as pl
from jax.experimental.pallas import tpu as pltpu

D_MODEL = 1024
DEPTH = 4
E_INNER = 2 * D_MODEL
HEAD_DIM = 128
E_A = E_INNER // 2
E_B = E_INNER - E_A
H_B = E_B // HEAD_DIM
CONV_WIDTH = 3
CHUNK = 128
AB_IN = 4 * E_A + 3 * E_B
E_C = E_INNER
POOL_WINDOWS = (2, 4, 8, 16)
G_C = E_C // len(POOL_WINDOWS)
EPS = 1e-6

SEQ_BLOCK = 1024
SEQ_TILE = 512
COL_BLOCK = 256
CONV_HALO = 8
POOL_HALO = 16
PREPARE_STEPS = 16
V7X_VMEM_BYTES = 64 * 1024 * 1024
VMEM_LIMIT_BYTES = V7X_VMEM_BYTES - 8 * 1024 * 1024

assert all(w == 2 ** (i + 1) for i, w in enumerate(POOL_WINDOWS))
assert POOL_WINDOWS[-1] <= POOL_HALO and CONV_WIDTH - 1 <= CONV_HALO

_F32 = jnp.float32
_BF16 = jnp.bfloat16


_dot = functools.partial(jnp.dot, preferred_element_type=_F32)


def _silu(v):
    return v * jax.nn.sigmoid(v)


def _rmsnorm(x, g):
    return x * lax.rsqrt(jnp.mean(x * x, axis=-1, keepdims=True) + EPS) * g


def _resident(block_shape, index_map):
    return pl.BlockSpec(block_shape, index_map, pipeline_mode=pl.Buffered(1))


def _layer_rows(mod_ref, g_ref, final_g_ref, layer):
    b = pl.program_id(0)
    mod = [mod_ref[k, pl.ds(b, 1), :] for k in range(3)]
    final_g = None if final_g_ref is None else final_g_ref[...].reshape(1, D_MODEL)
    return g_ref[layer:layer + 1, :], mod, final_g


def _modulated_input(x, g, mod):
    return _rmsnorm(x, g) * (1.0 + mod[1]) + mod[0]


def _project_out(x_ref, y_scr, w_out_ref, rows, mod, final_g, o_ref):
    out = None
    for k0 in range(0, y_scr.shape[1], COL_BLOCK):
        part = _dot(y_scr[rows, k0:k0 + COL_BLOCK],
                    _weight(w_out_ref, k0, k0 + COL_BLOCK, slice(None)))
        out = part if out is None else out + part
    res = x_ref[rows] + mod[2] * out
    if final_g is not None:
        res = _rmsnorm(res, final_g)
    o_ref[rows] = res


def _split_refs(rest, final_norm, n_cast):
    rest = list(rest)
    final_g_ref = rest.pop(0) if final_norm else None
    cast_src = [rest.pop(0) for _ in range(n_cast)]
    o_ref = rest.pop(0)
    cast_dst = [rest.pop(0) for _ in range(n_cast)]
    return final_g_ref, cast_src, o_ref, cast_dst, rest


def _pack_weight(w):
    return pltpu.bitcast(w.astype(_BF16), jnp.uint32)


def _weight(w_ref, k0, k1, cols):
    return pltpu.bitcast(w_ref[k0 // 2:k1 // 2, cols], _BF16)


def _cast_next_weights(cast_src, cast_dst):
    for src, dst in zip(cast_src, cast_dst):
        dst[...] = _pack_weight(src[...])


def _shift_rows(ext, k, halo):
    return pltpu.roll(ext, k, axis=0)[halo:]


def _tiles(x_ref):
    n_tiles = x_ref.shape[0] // SEQ_TILE
    return [slice(i * SEQ_TILE, (i + 1) * SEQ_TILE) for i in range(n_tiles)]


def _run_tiles(stages):
    stages[0][0]()
    for ti, (_, body, finish) in enumerate(stages):
        body()
        if ti + 1 < len(stages):
            stages[ti + 1][0]()
        finish()


def _even_kernel(x_ref, xn_ref, mod_ref, g_ref, w_in_ref, conv_ref, lng_ref, lnb_ref,
                 sgw_ref, sgb_ref, w_out_ref, *rest, layer, final_norm, n_cast):
    final_g_ref, cast_src, o_ref, cast_dst, scratch = _split_refs(rest, final_norm, n_cast)
    h_scr, y_scr, tail_scr, hn_scr, bv_scr = scratch
    g, mod, final_g = _layer_rows(mod_ref, g_ref, final_g_ref, layer)
    j = layer // 2
    ts = SEQ_TILE
    n_chunks = ts // CHUNK
    b = pl.program_id(0)
    s = pl.program_id(1)
    base_b = 4 * E_A
    lead_cols = slice(base_b + E_B, base_b + E_B + 2 * COL_BLOCK)

    def lead_in(x, mod_rows):
        hn_scr[...] = _modulated_input(x, g, mod_rows).astype(_BF16)
        bv_scr[...] = _dot(hn_scr[...], _weight(w_in_ref, 0, D_MODEL, lead_cols))

    @pl.when((b == 0) & (s == 0))
    def _():
        tail_scr[...] = jnp.zeros(tail_scr.shape, _F32)
        lead_in(x_ref[0:ts], mod)

    row = lax.broadcasted_iota(jnp.int32, (CHUNK, CHUNK), 0)
    col = lax.broadcasted_iota(jnp.int32, (CHUNK, CHUNK), 1)
    causal = row >= col
    heads_per_block = COL_BLOCK // HEAD_DIM
    n_blocks = E_B // COL_BLOCK
    bias_cols = []
    for head in range(H_B):
        b_row = jnp.broadcast_to(sgb_ref[head:head + 1, :], (CHUNK, CHUNK))
        bias_cols.append(jnp.concatenate([b_row.T] * n_chunks, axis=1))

    def mix_tile(ti, rows):
        def proj(seg_start, c0, width=COL_BLOCK):
            cols = slice(seg_start + c0, seg_start + c0 + width)
            h = hn_scr[...] if ti == 0 else h_scr[rows]
            return _dot(h, _weight(w_in_ref, 0, D_MODEL, cols))

        def a_block(c0):
            cols = slice(c0, c0 + COL_BLOCK)
            a_h = proj(0 * E_A, c0)
            a_c = proj(2 * E_A, c0)
            a_b = proj(1 * E_A, c0)
            a_z = proj(3 * E_A, c0)
            ch = a_c * a_h
            halo = tail_scr[:, cols]
            if ti == 0:
                halo = jnp.where(s > 0, halo, 0.0)
            tail_scr[:, cols] = ch[ts - CONV_HALO:]
            ext = jnp.concatenate([halo, ch], axis=0)
            w = conv_ref[:, cols]
            conv = (_shift_rows(ext, 2, CONV_HALO) * w[0:1]
                    + _shift_rows(ext, 1, CONV_HALO) * w[1:2] + ch * w[2:3])
            y_scr[rows, cols] = (a_b * conv * _silu(a_z)).astype(_BF16)

        def ln_blocks(c0, n_blk, b_v=None):
            if b_v is None:
                b_v = proj(base_b + 1 * E_B, c0, n_blk * COL_BLOCK)
            rhs = []
            for hh in range(n_blk * heads_per_block):
                glanes = slice(c0 + hh * HEAD_DIM, c0 + (hh + 1) * HEAD_DIM)
                v = b_v[:, hh * HEAD_DIM:(hh + 1) * HEAD_DIM]
                dev = v - jnp.mean(v, axis=-1, keepdims=True)
                var = jnp.mean(dev * dev, axis=-1, keepdims=True)
                vn = (dev * lax.rsqrt(var + EPS) * lng_ref[j:j + 1, glanes]
                      + lnb_ref[j:j + 1, glanes])
                vn = vn.astype(_BF16)
                rhs.append(jnp.concatenate(
                    [vn[n * CHUNK:(n + 1) * CHUNK] for n in range(n_chunks)], axis=1))
            return [rhs[i * heads_per_block:(i + 1) * heads_per_block] for i in range(n_blk)]

        def sgu_block(c0, rhs):
            mixed_heads = []
            for hh in range(heads_per_block):
                head = c0 // HEAD_DIM + hh
                w_h = jnp.where(causal, sgw_ref[head], 0.0).astype(_BF16)
                mixed = _dot(w_h, rhs[hh]) + bias_cols[head]
                mixed_heads.append(jnp.concatenate(
                    [mixed[:, n * CHUNK:(n + 1) * CHUNK] for n in range(n_chunks)], axis=0))
            mixed = jnp.concatenate(mixed_heads, axis=1)
            b_u = proj(base_b + 0 * E_B, c0)
            b_z = proj(base_b + 2 * E_B, c0)
            y_scr[rows, E_A + c0:E_A + c0 + COL_BLOCK] = (
                b_u * mixed * _silu(b_z)).astype(_BF16)

        rhs = []

        def head():
            if ti == 0:
                rhs.extend(ln_blocks(0, 2, bv_scr[...]))
            else:
                h_scr[rows] = _modulated_input(x_ref[rows], g, mod).astype(_BF16)
                rhs.extend(ln_blocks(0, 2))

        def body():
            for k in range(n_blocks):
                a_block(k * COL_BLOCK)
                if k + 2 < n_blocks:
                    rhs.extend(ln_blocks((k + 2) * COL_BLOCK, 1))
                sgu_block(k * COL_BLOCK, rhs[k])

        def finish():
            _project_out(x_ref, y_scr, w_out_ref, rows, mod, final_g, o_ref)

        return head, body, finish

    _cast_next_weights(cast_src, cast_dst)
    _run_tiles([mix_tile(ti, rows) for ti, rows in enumerate(_tiles(x_ref))])
    n_seq = pl.num_programs(1)
    nxt = jnp.minimum(b * n_seq + s + 1, pl.num_programs(0) * n_seq - 1)
    lead_in(xn_ref[...], [mod_ref[k, pl.ds(nxt // n_seq, 1), :] for k in range(3)])


def _prepare_kernel(c_ref, ada_w_ref, ada_b_ref, wp_ref, pw_ref, ps_ref, *rest, n_mod_units,
                    n_fold_units):
    n_cast = (len(rest) - 2) // 2
    cast_src, cast_dst = rest[:n_cast], rest[n_cast + 2:]
    mod_ref, wmix_ref = rest[n_cast:n_cast + 2]
    step = pl.program_id(0)

    @pl.when(step < n_mod_units)
    def _():
        c_act = _silu(c_ref[...]).astype(_BF16)
        bias = ada_b_ref[pl.ds(step // 3, 1), :]
        mod_ref[...] = _dot(c_act, ada_w_ref[...].astype(_BF16)) + bias

    @pl.when(step < n_fold_units)
    def _():
        def split(v):
            hi = v.astype(_BF16)
            return hi, (v - hi.astype(_F32)).astype(_BF16)

        a_hi, a_lo = split(wp_ref[...])
        b_hi, b_lo = split(pw_ref[...])
        folded = _dot(a_hi, b_hi) + (_dot(a_hi, b_lo) + _dot(a_lo, b_hi))
        scale = ps_ref[pl.ds(step // len(POOL_WINDOWS), 1), :]
        wmix_ref[...] = _pack_weight(folded * scale)

    _cast_next_weights(cast_src, cast_dst)


def _prepare_params(c, ada_w, ada_b, c_w_in, c_pool_w, c_pool_scale, cast_jobs):
    batch = c.shape[0]
    n_odd = c_w_in.shape[0]
    n_groups = len(POOL_WINDOWS)
    n_mod_units, n_fold_units = DEPTH * 3, n_odd * n_groups
    n_steps = PREPARE_STEPS
    assert n_steps >= max(n_mod_units, n_fold_units)

    def mod_unit(i):
        u = jnp.minimum(i, n_mod_units - 1)
        return u // 3, u % 3

    def fold_unit(i):
        u = jnp.minimum(i, n_fold_units - 1)
        return u // n_groups, u % n_groups

    c_in, c_args, c_out, c_shape = _cast_job_specs(cast_jobs, n_steps, lambda i: i)
    return pl.pallas_call(
        functools.partial(_prepare_kernel, n_mod_units=n_mod_units,
                          n_fold_units=n_fold_units),
        grid=(n_steps,),
        in_specs=[
            pl.BlockSpec((batch, D_MODEL), lambda i: (0, 0)),
            pl.BlockSpec((None, D_MODEL, D_MODEL),
                         lambda i: (mod_unit(i)[0], 0, mod_unit(i)[1])),
            pl.BlockSpec((DEPTH, D_MODEL), lambda i: (0, mod_unit(i)[1])),
            pl.BlockSpec((None, D_MODEL, G_C),
                         lambda i: (fold_unit(i)[0], 0, fold_unit(i)[1])),
            pl.BlockSpec((None, None, G_C, G_C), lambda i: (*fold_unit(i), 0, 0)),
            pl.BlockSpec((n_odd, G_C), lambda i: (0, fold_unit(i)[1])),
        ] + c_in,
        out_specs=[
            pl.BlockSpec((None, None, batch, D_MODEL), lambda i: (*mod_unit(i), 0, 0)),
            pl.BlockSpec((None, D_MODEL // 2, G_C),
                         lambda i: (fold_unit(i)[0], 0, fold_unit(i)[1])),
        ] + c_out,
        out_shape=[
            jax.ShapeDtypeStruct((DEPTH, 3, batch, D_MODEL), _F32),
            jax.ShapeDtypeStruct((n_odd, D_MODEL // 2, E_C), jnp.uint32),
        ] + c_shape,
        compiler_params=pltpu.CompilerParams(
            dimension_semantics=("arbitrary",),
            vmem_limit_bytes=VMEM_LIMIT_BYTES),
        name="prepare_params",
    )(c, ada_w, ada_b, c_w_in, c_pool_w, c_pool_scale, *c_args)


def _odd_kernel(x_ref, mod_ref, g_ref, w_mix_ref, w_z_ref, w_out_ref, *rest, layer,
                final_norm, n_cast):
    final_g_ref, cast_src, o_ref, cast_dst, scratch = _split_refs(rest, final_norm, n_cast)
    h_scr, y_scr, tail_scr, lh_scr = scratch
    g, mod, final_g = _layer_rows(mod_ref, g_ref, final_g_ref, layer)
    ts = SEQ_TILE
    b = pl.program_id(0)
    s = pl.program_id(1)
    n_groups = len(POOL_WINDOWS)
    tiles = _tiles(x_ref)

    @pl.when((b == 0) & (s == 0))
    def _():
        tail_scr[...] = jnp.zeros(tail_scr.shape, _F32)

    def window_sums(level, prev):
        sums, new_prev, k = [], [], 1
        for li, win in enumerate(POOL_WINDOWS):
            new_prev.append(level)
            before = level if prev is None else prev[li]
            ext = jnp.concatenate([before[POOL_HALO - 8:], level], axis=0)
            shifted = ext[8 - k:8 - k + POOL_HALO] if k == 8 else _shift_rows(ext, k, 8)
            level = level + shifted
            sums.append(level)
            k *= 2
        return sums, new_prev

    def prepare(ti, rows):
        halo = tail_scr[...]
        if ti == 0:
            halo = jnp.where(s > 0, halo, 0.0)
        _, prev = window_sums(halo, None)
        first_pos = (s * len(tiles) + ti) * ts
        for r0 in range(0, ts, POOL_HALO):
            grp = slice(rows.start + r0, rows.start + r0 + POOL_HALO)
            h = _modulated_input(x_ref[grp], g, mod)
            h_scr[grp] = h.astype(_BF16)
            sums, prev = window_sums(h, prev)
            pos = (first_pos + r0
                   + lax.broadcasted_iota(jnp.int32, (POOL_HALO, HEAD_DIM), 0))
            for gi, win in enumerate(POOL_WINDOWS):
                cnt = jnp.minimum(pos + 1, win).astype(_F32)
                inv_cnt = jnp.concatenate([1.0 / cnt] * (D_MODEL // HEAD_DIM), axis=1)
                lh_scr[gi, grp] = (sums[gi] * inv_cnt - h).astype(_BF16)
        tail_scr[...] = h

    def mix_tile(ti, rows):
        def gate_group(gi):
            gcols = slice(gi * G_C, (gi + 1) * G_C)
            return _silu(_dot(h_scr[rows], _weight(w_z_ref, 0, D_MODEL, gcols)))

        def mix_group(gi, gate):
            gcols = slice(gi * G_C, (gi + 1) * G_C)
            mixed = _dot(lh_scr[gi, rows], _weight(w_mix_ref, 0, D_MODEL, gcols))
            y_scr[rows, gcols] = (mixed * gate).astype(_BF16)

        lead = min(2, n_groups)
        gates = []

        def head():
            prepare(ti, rows)
            gates.extend(gate_group(gi) for gi in range(lead))
            mix_group(0, gates[0])

        def body():
            for gi in range(1, n_groups):
                if gi - 1 + lead < n_groups:
                    gates.append(gate_group(gi - 1 + lead))
                mix_group(gi, gates[gi])

        def finish():
            _project_out(x_ref, y_scr, w_out_ref, rows, mod, final_g, o_ref)

        return head, body, finish

    _cast_next_weights(cast_src, cast_dst)
    _run_tiles([mix_tile(ti, rows) for ti, rows in enumerate(tiles)])


def _cast_job_specs(cast_jobs, n_steps, step_of):
    in_specs, args, out_specs, out_shape = [], [], [], []
    for src, j, col_block, n_cols in cast_jobs:
        n_rows = src.shape[1]
        rows = n_rows // n_steps
        assert rows * n_steps == n_rows and rows % 16 == 0
        in_specs.append(pl.BlockSpec(
            (None, rows, n_cols),
            lambda *g, j=j, cb=col_block: (j, step_of(*g), cb)))
        args.append(src)
        out_specs.append(pl.BlockSpec((rows // 2, n_cols), lambda *g: (step_of(*g), 0)))
        out_shape.append(jax.ShapeDtypeStruct((n_rows // 2, n_cols), jnp.uint32))
    return in_specs, args, out_specs, out_shape


def _layer_call(kernel_fn, name, layer, x, mod, norm_g, weights, weight_specs,
                scratch_shapes, final_g, cast_jobs, lead_in):
    batch, seq, _ = x.shape
    n_seq = seq // SEQ_BLOCK
    n_steps = batch * n_seq
    x_spec = pl.BlockSpec((None, SEQ_BLOCK, D_MODEL), lambda b, s: (b, s, 0))

    def next_first_tile(b, s):
        nxt = jnp.minimum(b * n_seq + s + 1, n_steps - 1)
        return nxt // n_seq, (nxt % n_seq) * (SEQ_BLOCK // SEQ_TILE), 0

    in_specs = [x_spec]
    args = [x]
    if lead_in:
        in_specs.append(pl.BlockSpec((None, SEQ_TILE, D_MODEL), next_first_tile))
        args.append(x)
    in_specs += [
        _resident((None, 3, batch, D_MODEL), lambda b, s: (layer, 0, 0, 0)),
        _resident((DEPTH, D_MODEL), lambda b, s: (0, 0)),
    ] + weight_specs
    args += [mod, norm_g] + weights
    if final_g is not None:
        in_specs.append(_resident((D_MODEL,), lambda b, s: (0,)))
        args.append(final_g)
    c_in, c_args, c_out, c_shape = _cast_job_specs(
        cast_jobs, n_steps, lambda b, s: b * n_seq + s)
    in_specs += c_in
    args += c_args
    out_specs = [x_spec] + c_out
    out_shape = [jax.ShapeDtypeStruct(x.shape, x.dtype)] + c_shape
    return pl.pallas_call(
        functools.partial(kernel_fn, layer=layer, final_norm=final_g is not None,
                          n_cast=len(cast_jobs)),
        grid=(batch, n_seq),
        in_specs=in_specs,
        out_specs=out_specs,
        out_shape=out_shape,
        scratch_shapes=[
            pltpu.VMEM((SEQ_BLOCK, D_MODEL), _BF16),
            pltpu.VMEM((SEQ_BLOCK, E_INNER), _BF16),
        ] + scratch_shapes,
        compiler_params=pltpu.CompilerParams(
            dimension_semantics=("arbitrary", "arbitrary"),
            vmem_limit_bytes=VMEM_LIMIT_BYTES),
        name=name,
    )(*args)


def _even_layer(layer, x, mod, norm_g, w_in, conv_w, ln_g, ln_b, sgu_w, sgu_b, w_out,
                final_g, cast_jobs):
    j = layer // 2
    n_even = conv_w.shape[0]
    const2 = lambda b, s: (0, 0)
    sel2 = lambda b, s: (j, 0, 0)
    sel3 = lambda b, s: (j, 0, 0, 0)
    weights = [w_in, conv_w, ln_g, ln_b, sgu_w, sgu_b, w_out]
    specs = [
        _resident((D_MODEL // 2, AB_IN), const2),
        _resident((None, CONV_WIDTH, E_A), sel2),
        _resident((n_even, E_B), const2),
        _resident((n_even, E_B), const2),
        _resident((None, H_B, CHUNK, CHUNK), sel3),
        _resident((None, H_B, CHUNK), sel2),
        _resident((E_INNER // 2, D_MODEL), const2),
    ]
    scratch = [
        pltpu.VMEM((CONV_HALO, E_A), _F32),
        pltpu.VMEM((SEQ_TILE, D_MODEL), _BF16),
        pltpu.VMEM((SEQ_TILE, 2 * COL_BLOCK), _F32),
    ]
    return _layer_call(_even_kernel, "even_layer", layer, x, mod, norm_g, weights, specs,
                       scratch, final_g, cast_jobs, True)


def _odd_layer(layer, x, mod, norm_g, w_mix, w_z, w_out, final_g, cast_jobs):
    j = layer // 2
    const2 = lambda b, s: (0, 0)
    weights = [w_mix, w_z, w_out]
    specs = [
        _resident((None, D_MODEL // 2, E_C), lambda b, s: (j, 0, 0)),
        _resident((D_MODEL // 2, E_C), const2),
        _resident((E_C // 2, D_MODEL), const2),
    ]
    scratch = [
        pltpu.VMEM((POOL_HALO, D_MODEL), _F32),
        pltpu.VMEM((len(POOL_WINDOWS), SEQ_BLOCK, D_MODEL), _BF16),
    ]
    return _layer_call(_odd_kernel, "odd_layer", layer, x, mod, norm_g, weights, specs,
                       scratch, final_g, cast_jobs, False)


def kernel(x, c, norm_g, ada_w, ada_b, ab_w_in, ab_conv_w, ab_ln_g, ab_ln_b, ab_sgu_w,
           ab_sgu_b, ab_w_out, c_w_in, c_pool_w, c_pool_scale, c_w_out, final_g):
    batch = x.shape[0]
    assert x.shape[1] % SEQ_BLOCK == 0 and SEQ_BLOCK % SEQ_TILE == 0
    assert SEQ_TILE % CHUNK == 0
    mod, c_w_mix, *big_w = _prepare_params(
        c, ada_w, ada_b, c_w_in, c_pool_w, c_pool_scale,
        [(ab_w_in, 0, 0, AB_IN), (ab_w_out, 0, 0, D_MODEL)])
    for i in range(DEPTH):
        fg = final_g if i == DEPTH - 1 else None
        nxt = (i + 1) // 2
        if i + 1 == DEPTH:
            cast_jobs = []
        elif i % 2 == 0:
            cast_jobs = [(c_w_in, nxt, 1, E_C), (c_w_out, nxt, 0, D_MODEL)]
        else:
            cast_jobs = [(ab_w_in, nxt, 0, AB_IN), (ab_w_out, nxt, 0, D_MODEL)]
        if i % 2 == 0:
            x, *big_w = _even_layer(i, x, mod, norm_g, big_w[0], ab_conv_w, ab_ln_g,
                                    ab_ln_b, ab_sgu_w, ab_sgu_b, big_w[1], fg, cast_jobs)
        else:
            x, *big_w = _odd_layer(i, x, mod, norm_g, c_w_mix, big_w[0], big_w[1], fg,
                                   cast_jobs)
    return x
```

```python
import functools

import jax
import jax.numpy as jnp
from jax import lax
from jax.experimental import pallas as pl
from jax.experimental.pallas import tpu as pltpu

D_MODEL = 1024
DEPTH = 4
E_INNER = 2 * D_MODEL
HEAD_DIM = 128
E_A = E_INNER // 2
E_B = E_INNER - E_A
H_B = E_B // HEAD_DIM
CONV_WIDTH = 3
CHUNK = 128
AB_IN = 4 * E_A + 3 * E_B
E_C = E_INNER
POOL_WINDOWS = (2, 4, 8, 16)
G_C = E_C // len(POOL_WINDOWS)
EPS = 1e-6

SEQ_BLOCK = 1024
SEQ_TILE = 512
COL_BLOCK = 256
CONV_HALO = 8
POOL_HALO = 16
PREPARE_STEPS = 12
FETCH_ROWS = 64
V7X_VMEM_BYTES = 64 * 1024 * 1024
VMEM_LIMIT_BYTES = V7X_VMEM_BYTES - 8 * 1024 * 1024

assert all(w == 2 ** (i + 1) for i, w in enumerate(POOL_WINDOWS))
assert POOL_WINDOWS[-1] <= POOL_HALO and CONV_WIDTH - 1 <= CONV_HALO

_F32 = jnp.float32
_BF16 = jnp.bfloat16


_dot = functools.partial(jnp.dot, preferred_element_type=_F32)


def _silu(v):
    return v * jax.nn.sigmoid(v)


def _rmsnorm(x, g):
    return x * lax.rsqrt(jnp.mean(x * x, axis=-1, keepdims=True) + EPS) * g


def _resident(block_shape, index_map):
    return pl.BlockSpec(block_shape, index_map, pipeline_mode=pl.Buffered(1))


def _layer_rows(mod_ref, g_ref, final_g_ref, layer):
    b = pl.program_id(0)
    mod = [mod_ref[k, pl.ds(b, 1), :] for k in range(3)]
    final_g = None if final_g_ref is None else final_g_ref[...].reshape(1, D_MODEL)
    return g_ref[layer:layer + 1, :], mod, final_g


def _modulated_input(x, g, mod):
    return _rmsnorm(x, g) * (1.0 + mod[1]) + mod[0]


def _project_out(x_ref, y_scr, w_out_ref, rows, mod, final_g, o_ref):
    out = None
    for k0 in range(0, y_scr.shape[1], COL_BLOCK):
        part = _dot(y_scr[rows, k0:k0 + COL_BLOCK],
                    _weight(w_out_ref, k0, k0 + COL_BLOCK, slice(None)))
        out = part if out is None else out + part
    res = x_ref[rows] + mod[2] * out
    if final_g is not None:
        res = _rmsnorm(res, final_g)
    o_ref[rows] = res


def _split_refs(rest, final_norm, n_cast):
    rest = list(rest)
    final_g_ref = rest.pop(0) if final_norm else None
    cast_src = [rest.pop(0) for _ in range(n_cast)]
    o_ref = rest.pop(0)
    cast_dst = [rest.pop(0) for _ in range(n_cast)]
    return final_g_ref, cast_src, o_ref, cast_dst, rest


def _pack_weight(w):
    return pltpu.bitcast(w.astype(_BF16), jnp.uint32)


def _weight(w_ref, k0, k1, cols):
    return pltpu.bitcast(w_ref[k0 // 2:k1 // 2, cols], _BF16)


def _cast_next_weights(cast_src, cast_dst):
    for src, dst in zip(cast_src, cast_dst):
        dst[...] = _pack_weight(src[...])


def _fetch_packed(src_hbm, stage, sem, dst):
    rows = stage.shape[1]
    n_chunks = src_hbm.shape[0] // rows

    def chunk_copy(i):
        return pltpu.make_async_copy(
            src_hbm.at[pl.ds(i * rows, rows), :], stage.at[i % 2], sem.at[i % 2])

    chunk_copy(0).start()
    for i in range(n_chunks):
        if i + 1 < n_chunks:
            chunk_copy(i + 1).start()
        chunk_copy(i).wait()
        dst[i * rows // 2:(i + 1) * rows // 2, :] = _pack_weight(stage[i % 2])


def _shift_rows(ext, k, halo):
    return pltpu.roll(ext, k, axis=0)[halo:]


def _tiles(x_ref):
    n_tiles = x_ref.shape[0] // SEQ_TILE
    return [slice(i * SEQ_TILE, (i + 1) * SEQ_TILE) for i in range(n_tiles)]


def _run_tiles(stages):
    stages[0][0]()
    for ti, (_, body, finish) in enumerate(stages):
        body()
        if ti + 1 < len(stages):
            stages[ti + 1][0]()
        finish()


def _even_kernel(x_ref, mod_ref, g_ref, w_in_ref, conv_ref, lng_ref, lnb_ref,
                 sgw_ref, sgb_ref, w_out_ref, *rest, layer, final_norm, n_cast,
                 own_weights):
    final_g_ref, cast_src, o_ref, cast_dst, scratch = _split_refs(rest, final_norm, n_cast)
    h_scr, y_scr, tail_scr = scratch[:3]
    g, mod, final_g = _layer_rows(mod_ref, g_ref, final_g_ref, layer)
    j = layer // 2
    ts = SEQ_TILE
    n_chunks = ts // CHUNK
    b = pl.program_id(0)
    s = pl.program_id(1)
    if own_weights:
        w_in_hbm, w_out_hbm = w_in_ref, w_out_ref
        w_in_ref, w_out_ref, stage_in, stage_out, sem = scratch[3:]

    @pl.when((b == 0) & (s == 0))
    def _():
        tail_scr[...] = jnp.zeros(tail_scr.shape, _F32)
        if own_weights:
            _fetch_packed(w_in_hbm.at[j], stage_in, sem, w_in_ref)
            _fetch_packed(w_out_hbm.at[j], stage_out, sem, w_out_ref)

    row = lax.broadcasted_iota(jnp.int32, (CHUNK, CHUNK), 0)
    col = lax.broadcasted_iota(jnp.int32, (CHUNK, CHUNK), 1)
    causal = row >= col
    base_b = 4 * E_A
    heads_per_block = COL_BLOCK // HEAD_DIM
    n_blocks = E_B // COL_BLOCK
    bias_cols = []
    for head in range(H_B):
        b_row = jnp.broadcast_to(sgb_ref[head:head + 1, :], (CHUNK, CHUNK))
        bias_cols.append(jnp.concatenate([b_row.T] * n_chunks, axis=1))

    def mix_tile(ti, rows):
        def proj(seg_start, c0, width=COL_BLOCK):
            cols = slice(seg_start + c0, seg_start + c0 + width)
            return _dot(h_scr[rows], _weight(w_in_ref, 0, D_MODEL, cols))

        def a_block(c0):
            cols = slice(c0, c0 + COL_BLOCK)
            a_h = proj(0 * E_A, c0)
            a_c = proj(2 * E_A, c0)
            a_b = proj(1 * E_A, c0)
            a_z = proj(3 * E_A, c0)
            ch = a_c * a_h
            halo = tail_scr[:, cols]
            if ti == 0:
                halo = jnp.where(s > 0, halo, 0.0)
            tail_scr[:, cols] = ch[ts - CONV_HALO:]
            ext = jnp.concatenate([halo, ch], axis=0)
            w = conv_ref[:, cols]
            conv = (_shift_rows(ext, 2, CONV_HALO) * w[0:1]
                    + _shift_rows(ext, 1, CONV_HALO) * w[1:2] + ch * w[2:3])
            y_scr[rows, cols] = (a_b * conv * _silu(a_z)).astype(_BF16)

        def ln_blocks(c0, n_blk):
            b_v = proj(base_b + 1 * E_B, c0, n_blk * COL_BLOCK)
            rhs = []
            for hh in range(n_blk * heads_per_block):
                glanes = slice(c0 + hh * HEAD_DIM, c0 + (hh + 1) * HEAD_DIM)
                v = b_v[:, hh * HEAD_DIM:(hh + 1) * HEAD_DIM]
                dev = v - jnp.mean(v, axis=-1, keepdims=True)
                var = jnp.mean(dev * dev, axis=-1, keepdims=True)
                vn = (dev * lax.rsqrt(var + EPS) * lng_ref[j:j + 1, glanes]
                      + lnb_ref[j:j + 1, glanes])
                vn = vn.astype(_BF16)
                rhs.append(jnp.concatenate(
                    [vn[n * CHUNK:(n + 1) * CHUNK] for n in range(n_chunks)], axis=1))
            return [rhs[i * heads_per_block:(i + 1) * heads_per_block] for i in range(n_blk)]

        def sgu_block(c0, rhs):
            mixed_heads = []
            for hh in range(heads_per_block):
                head = c0 // HEAD_DIM + hh
                w_h = jnp.where(causal, sgw_ref[head], 0.0).astype(_BF16)
                mixed = _dot(w_h, rhs[hh]) + bias_cols[head]
                mixed_heads.append(jnp.concatenate(
                    [mixed[:, n * CHUNK:(n + 1) * CHUNK] for n in range(n_chunks)], axis=0))
            mixed = jnp.concatenate(mixed_heads, axis=1)
            b_u = proj(base_b + 0 * E_B, c0)
            b_z = proj(base_b + 2 * E_B, c0)
            y_scr[rows, E_A + c0:E_A + c0 + COL_BLOCK] = (
                b_u * mixed * _silu(b_z)).astype(_BF16)

        rhs = []

        def head():
            h_scr[rows] = _modulated_input(x_ref[rows], g, mod).astype(_BF16)
            rhs.extend(ln_blocks(0, 2))

        def body():
            for k in range(n_blocks):
                a_block(k * COL_BLOCK)
                if k + 2 < n_blocks:
                    rhs.extend(ln_blocks((k + 2) * COL_BLOCK, 1))
                sgu_block(k * COL_BLOCK, rhs[k])

        def finish():
            _project_out(x_ref, y_scr, w_out_ref, rows, mod, final_g, o_ref)

        return head, body, finish

    _cast_next_weights(cast_src, cast_dst)
    _run_tiles([mix_tile(ti, rows) for ti, rows in enumerate(_tiles(x_ref))])


def _prepare_kernel(c_ref, ada_w_ref, ada_b_ref, wp_ref, pw_ref, ps_ref, *rest, n_mod_units,
                    n_fold_units):
    n_cast = (len(rest) - 2) // 2
    cast_src, cast_dst = rest[:n_cast], rest[n_cast + 2:]
    mod_ref, wmix_ref = rest[n_cast:n_cast + 2]
    step = pl.program_id(0)

    @pl.when(step < n_mod_units)
    def _():
        c_act = _silu(c_ref[...]).astype(_BF16)
        bias = ada_b_ref[pl.ds(step // 3, 1), :]
        mod_ref[...] = _dot(c_act, ada_w_ref[...].astype(_BF16)) + bias

    @pl.when(step < n_fold_units)
    def _():
        def split(v):
            hi = v.astype(_BF16)
            return hi, (v - hi.astype(_F32)).astype(_BF16)

        a_hi, a_lo = split(wp_ref[...])
        b_hi, b_lo = split(pw_ref[...])
        folded = _dot(a_hi, b_hi) + (_dot(a_hi, b_lo) + _dot(a_lo, b_hi))
        scale = ps_ref[pl.ds(step // len(POOL_WINDOWS), 1), :]
        wmix_ref[...] = _pack_weight(folded * scale)

    _cast_next_weights(cast_src, cast_dst)


def _prepare_params(c, ada_w, ada_b, c_w_in, c_pool_w, c_pool_scale, cast_jobs):
    batch = c.shape[0]
    n_odd = c_w_in.shape[0]
    n_groups = len(POOL_WINDOWS)
    n_mod_units, n_fold_units = DEPTH * 3, n_odd * n_groups
    n_steps = PREPARE_STEPS
    assert n_steps >= max(n_mod_units, n_fold_units)

    def mod_unit(i):
        u = jnp.minimum(i, n_mod_units - 1)
        return u // 3, u % 3

    def fold_unit(i):
        u = jnp.minimum(i, n_fold_units - 1)
        return u // n_groups, u % n_groups

    c_in, c_args, c_out, c_shape = _cast_job_specs(cast_jobs, n_steps, lambda i: i)
    return pl.pallas_call(
        functools.partial(_prepare_kernel, n_mod_units=n_mod_units,
                          n_fold_units=n_fold_units),
        grid=(n_steps,),
        in_specs=[
            pl.BlockSpec((batch, D_MODEL), lambda i: (0, 0)),
            pl.BlockSpec((None, D_MODEL, D_MODEL),
                         lambda i: (mod_unit(i)[0], 0, mod_unit(i)[1])),
            pl.BlockSpec((DEPTH, D_MODEL), lambda i: (0, mod_unit(i)[1])),
            pl.BlockSpec((None, D_MODEL, G_C),
                         lambda i: (fold_unit(i)[0], 0, fold_unit(i)[1])),
            pl.BlockSpec((None, None, G_C, G_C), lambda i: (*fold_unit(i), 0, 0)),
            pl.BlockSpec((n_odd, G_C), lambda i: (0, fold_unit(i)[1])),
        ] + c_in,
        out_specs=[
            pl.BlockSpec((None, None, batch, D_MODEL), lambda i: (*mod_unit(i), 0, 0)),
            pl.BlockSpec((None, D_MODEL // 2, G_C),
                         lambda i: (fold_unit(i)[0], 0, fold_unit(i)[1])),
        ] + c_out,
        out_shape=[
            jax.ShapeDtypeStruct((DEPTH, 3, batch, D_MODEL), _F32),
            jax.ShapeDtypeStruct((n_odd, D_MODEL // 2, E_C), jnp.uint32),
        ] + c_shape,
        compiler_params=pltpu.CompilerParams(
            dimension_semantics=("arbitrary",),
            vmem_limit_bytes=VMEM_LIMIT_BYTES),
        name="prepare_params",
    )(c, ada_w, ada_b, c_w_in, c_pool_w, c_pool_scale, *c_args)


def _odd_kernel(x_ref, mod_ref, g_ref, w_mix_ref, w_z_ref, w_out_ref, *rest, layer,
                final_norm, n_cast):
    final_g_ref, cast_src, o_ref, cast_dst, scratch = _split_refs(rest, final_norm, n_cast)
    h_scr, y_scr, tail_scr, lh_scr = scratch
    g, mod, final_g = _layer_rows(mod_ref, g_ref, final_g_ref, layer)
    ts = SEQ_TILE
    b = pl.program_id(0)
    s = pl.program_id(1)
    n_groups = len(POOL_WINDOWS)
    tiles = _tiles(x_ref)

    @pl.when((b == 0) & (s == 0))
    def _():
        tail_scr[...] = jnp.zeros(tail_scr.shape, _F32)

    def window_sums(level, prev):
        sums, new_prev, k = [], [], 1
        for li, win in enumerate(POOL_WINDOWS):
            new_prev.append(level)
            before = level if prev is None else prev[li]
            ext = jnp.concatenate([before[POOL_HALO - 8:], level], axis=0)
            shifted = ext[8 - k:8 - k + POOL_HALO] if k == 8 else _shift_rows(ext, k, 8)
            level = level + shifted
            sums.append(level)
            k *= 2
        return sums, new_prev

    def prepare(ti, rows):
        halo = tail_scr[...]
        if ti == 0:
            halo = jnp.where(s > 0, halo, 0.0)
        _, prev = window_sums(halo, None)
        first_pos = (s * len(tiles) + ti) * ts
        for r0 in range(0, ts, POOL_HALO):
            grp = slice(rows.start + r0, rows.start + r0 + POOL_HALO)
            h = _modulated_input(x_ref[grp], g, mod)
            h_scr[grp] = h.astype(_BF16)
            sums, prev = window_sums(h, prev)
            pos = (first_pos + r0
                   + lax.broadcasted_iota(jnp.int32, (POOL_HALO, HEAD_DIM), 0))
            for gi, win in enumerate(POOL_WINDOWS):
                cnt = jnp.minimum(pos + 1, win).astype(_F32)
                inv_cnt = jnp.concatenate([1.0 / cnt] * (D_MODEL // HEAD_DIM), axis=1)
                lh_scr[gi, grp] = (sums[gi] * inv_cnt - h).astype(_BF16)
        tail_scr[...] = h

    def mix_tile(ti, rows):
        def gate_group(gi):
            gcols = slice(gi * G_C, (gi + 1) * G_C)
            return _silu(_dot(h_scr[rows], _weight(w_z_ref, 0, D_MODEL, gcols)))

        def mix_group(gi, gate):
            gcols = slice(gi * G_C, (gi + 1) * G_C)
            mixed = _dot(lh_scr[gi, rows], _weight(w_mix_ref, 0, D_MODEL, gcols))
            y_scr[rows, gcols] = (mixed * gate).astype(_BF16)

        lead = min(2, n_groups)
        gates = []

        def head():
            prepare(ti, rows)
            gates.extend(gate_group(gi) for gi in range(lead))
            mix_group(0, gates[0])

        def body():
            for gi in range(1, n_groups):
                if gi - 1 + lead < n_groups:
                    gates.append(gate_group(gi - 1 + lead))
                mix_group(gi, gates[gi])

        def finish():
            _project_out(x_ref, y_scr, w_out_ref, rows, mod, final_g, o_ref)

        return head, body, finish

    _cast_next_weights(cast_src, cast_dst)
    _run_tiles([mix_tile(ti, rows) for ti, rows in enumerate(tiles)])


def _cast_job_specs(cast_jobs, n_steps, step_of):
    in_specs, args, out_specs, out_shape = [], [], [], []
    for src, j, col_block, n_cols in cast_jobs:
        n_rows = src.shape[1]
        rows = n_rows // n_steps
        assert rows * n_steps == n_rows and rows % 16 == 0
        in_specs.append(pl.BlockSpec(
            (None, rows, n_cols),
            lambda *g, j=j, cb=col_block: (j, step_of(*g), cb)))
        args.append(src)
        out_specs.append(pl.BlockSpec((rows // 2, n_cols), lambda *g: (step_of(*g), 0)))
        out_shape.append(jax.ShapeDtypeStruct((n_rows // 2, n_cols), jnp.uint32))
    return in_specs, args, out_specs, out_shape


def _layer_call(kernel_fn, name, layer, x, mod, norm_g, weights, weight_specs,
                scratch_shapes, final_g, cast_jobs):
    batch, seq, _ = x.shape
    n_seq = seq // SEQ_BLOCK
    n_steps = batch * n_seq
    x_spec = pl.BlockSpec((None, SEQ_BLOCK, D_MODEL), lambda b, s: (b, s, 0))
    in_specs = [
        x_spec,
        _resident((None, 3, batch, D_MODEL), lambda b, s: (layer, 0, 0, 0)),
        _resident((DEPTH, D_MODEL), lambda b, s: (0, 0)),
    ] + weight_specs
    args = [x, mod, norm_g] + weights
    if final_g is not None:
        in_specs.append(_resident((D_MODEL,), lambda b, s: (0,)))
        args.append(final_g)
    c_in, c_args, c_out, c_shape = _cast_job_specs(
        cast_jobs, n_steps, lambda b, s: b * n_seq + s)
    in_specs += c_in
    args += c_args
    out_specs = [x_spec] + c_out
    out_shape = [jax.ShapeDtypeStruct(x.shape, x.dtype)] + c_shape
    return pl.pallas_call(
        functools.partial(kernel_fn, layer=layer, final_norm=final_g is not None,
                          n_cast=len(cast_jobs)),
        grid=(batch, n_seq),
        in_specs=in_specs,
        out_specs=out_specs,
        out_shape=out_shape,
        scratch_shapes=[
            pltpu.VMEM((SEQ_BLOCK, D_MODEL), _BF16),
            pltpu.VMEM((SEQ_BLOCK, E_INNER), _BF16),
        ] + scratch_shapes,
        compiler_params=pltpu.CompilerParams(
            dimension_semantics=("arbitrary", "arbitrary"),
            vmem_limit_bytes=VMEM_LIMIT_BYTES),
        name=name,
    )(*args)


def _even_layer(layer, x, mod, norm_g, w_in, conv_w, ln_g, ln_b, sgu_w, sgu_b, w_out,
                final_g, cast_jobs, own_weights):
    j = layer // 2
    n_even = conv_w.shape[0]
    const2 = lambda b, s: (0, 0)
    sel2 = lambda b, s: (j, 0, 0)
    sel3 = lambda b, s: (j, 0, 0, 0)
    weights = [w_in, conv_w, ln_g, ln_b, sgu_w, sgu_b, w_out]
    in_hbm = pl.BlockSpec(memory_space=pl.ANY)
    specs = [
        in_hbm if own_weights else _resident((D_MODEL // 2, AB_IN), const2),
        _resident((None, CONV_WIDTH, E_A), sel2),
        _resident((n_even, E_B), const2),
        _resident((n_even, E_B), const2),
        _resident((None, H_B, CHUNK, CHUNK), sel3),
        _resident((None, H_B, CHUNK), sel2),
        in_hbm if own_weights else _resident((E_INNER // 2, D_MODEL), const2),
    ]
    scratch = [pltpu.VMEM((CONV_HALO, E_A), _F32)]
    if own_weights:
        scratch += [
            pltpu.VMEM((D_MODEL // 2, AB_IN), jnp.uint32),
            pltpu.VMEM((E_INNER // 2, D_MODEL), jnp.uint32),
            pltpu.VMEM((2, FETCH_ROWS, AB_IN), _F32),
            pltpu.VMEM((2, 4 * FETCH_ROWS, D_MODEL), _F32),
            pltpu.SemaphoreType.DMA((2,)),
        ]
    return _layer_call(functools.partial(_even_kernel, own_weights=own_weights),
                       "even_layer", layer, x, mod, norm_g, weights, specs,
                       scratch, final_g, cast_jobs)


def _odd_layer(layer, x, mod, norm_g, w_mix, w_z, w_out, final_g, cast_jobs):
    j = layer // 2
    const2 = lambda b, s: (0, 0)
    weights = [w_mix, w_z, w_out]
    specs = [
        _resident((None, D_MODEL // 2, E_C), lambda b, s: (j, 0, 0)),
        _resident((D_MODEL // 2, E_C), const2),
        _resident((E_C // 2, D_MODEL), const2),
    ]
    scratch = [
        pltpu.VMEM((POOL_HALO, D_MODEL), _F32),
        pltpu.VMEM((len(POOL_WINDOWS), SEQ_BLOCK, D_MODEL), _BF16),
    ]
    return _layer_call(_odd_kernel, "odd_layer", layer, x, mod, norm_g, weights, specs,
                       scratch, final_g, cast_jobs)


def kernel(x, c, norm_g, ada_w, ada_b, ab_w_in, ab_conv_w, ab_ln_g, ab_ln_b, ab_sgu_w,
           ab_sgu_b, ab_w_out, c_w_in, c_pool_w, c_pool_scale, c_w_out, final_g):
    batch = x.shape[0]
    assert x.shape[1] % SEQ_BLOCK == 0 and SEQ_BLOCK % SEQ_TILE == 0
    assert SEQ_TILE % CHUNK == 0
    mod, c_w_mix = _prepare_params(c, ada_w, ada_b, c_w_in, c_pool_w, c_pool_scale, [])
    big_w = [ab_w_in, ab_w_out]
    for i in range(DEPTH):
        fg = final_g if i == DEPTH - 1 else None
        nxt = (i + 1) // 2
        if i + 1 == DEPTH:
            cast_jobs = []
        elif i % 2 == 0:
            cast_jobs = [(c_w_in, nxt, 1, E_C), (c_w_out, nxt, 0, D_MODEL)]
        else:
            cast_jobs = [(ab_w_in, nxt, 0, AB_IN), (ab_w_out, nxt, 0, D_MODEL)]
        if i % 2 == 0:
            x, *big_w = _even_layer(i, x, mod, norm_g, big_w[0], ab_conv_w, ab_ln_g,
                                    ab_ln_b, ab_sgu_w, ab_sgu_b, big_w[1], fg, cast_jobs,
                                    own_weights=i == 0)
        else:
            x, *big_w = _odd_layer(i, x, mod, norm_g, c_w_mix, big_w[0], big_w[1], fg,
                                   cast_jobs)
    return x
```

```python
import functools

import jax
import jax.numpy as jnp
from jax import lax
from jax.experimental import pallas as pl
from jax.experimental.pallas import tpu as pltpu

D_MODEL = 1024
DEPTH = 4
E_INNER = 2 * D_MODEL
HEAD_DIM = 128
E_A = E_INNER // 2
E_B = E_INNER - E_A
H_B = E_B // HEAD_DIM
CONV_WIDTH = 3
CHUNK = 128
AB_IN = 4 * E_A + 3 * E_B
E_C = E_INNER
POOL_WINDOWS = (2, 4, 8, 16)
G_C = E_C // len(POOL_WINDOWS)
EPS = 1e-6

SEQ_BLOCK = 1024
SEQ_TILE = 512
COL_BLOCK = 256
CONV_HALO = 8
POOL_HALO = 16
PREPARE_STEPS = 12
FETCH_ROWS = 64
FETCH_SLOTS = 4
V7X_VMEM_BYTES = 64 * 1024 * 1024
VMEM_LIMIT_BYTES = V7X_VMEM_BYTES - 8 * 1024 * 1024

assert all(w == 2 ** (i + 1) for i, w in enumerate(POOL_WINDOWS))
assert POOL_WINDOWS[-1] <= POOL_HALO and CONV_WIDTH - 1 <= CONV_HALO

_F32 = jnp.float32
_BF16 = jnp.bfloat16


_dot = functools.partial(jnp.dot, preferred_element_type=_F32)


def _silu(v):
    return v * jax.nn.sigmoid(v)


def _rmsnorm(x, g):
    return x * lax.rsqrt(jnp.mean(x * x, axis=-1, keepdims=True) + EPS) * g


def _resident(block_shape, index_map):
    return pl.BlockSpec(block_shape, index_map, pipeline_mode=pl.Buffered(1))


def _layer_rows(mod_ref, g_ref, final_g_ref, layer):
    b = pl.program_id(0)
    mod = [mod_ref[k, pl.ds(b, 1), :] for k in range(3)]
    final_g = None if final_g_ref is None else final_g_ref[...].reshape(1, D_MODEL)
    return g_ref[layer:layer + 1, :], mod, final_g


def _modulated_input(x, g, mod):
    return _rmsnorm(x, g) * (1.0 + mod[1]) + mod[0]


def _project_out(x_ref, y_scr, w_out_ref, rows, mod, final_g, o_ref):
    out = None
    for k0 in range(0, y_scr.shape[1], COL_BLOCK):
        part = _dot(y_scr[rows, k0:k0 + COL_BLOCK],
                    _weight(w_out_ref, k0, k0 + COL_BLOCK, slice(None)))
        out = part if out is None else out + part
    res = x_ref[rows] + mod[2] * out
    if final_g is not None:
        res = _rmsnorm(res, final_g)
    o_ref[rows] = res


def _split_refs(rest, final_norm, n_cast):
    rest = list(rest)
    final_g_ref = rest.pop(0) if final_norm else None
    cast_src = [rest.pop(0) for _ in range(n_cast)]
    o_ref = rest.pop(0)
    cast_dst = [rest.pop(0) for _ in range(n_cast)]
    return final_g_ref, cast_src, o_ref, cast_dst, rest


def _pack_weight(w):
    return pltpu.bitcast(w.astype(_BF16), jnp.uint32)


def _weight(w_ref, k0, k1, cols):
    return pltpu.bitcast(w_ref[k0 // 2:k1 // 2, cols], _BF16)


def _cast_next_weights(cast_src, cast_dst):
    for src, dst in zip(cast_src, cast_dst):
        dst[...] = _pack_weight(src[...])


def _fetch_packed(src_hbm, stage, sem, dst):
    depth, rows = stage.shape[0], stage.shape[1]
    n_chunks = src_hbm.shape[0] // rows

    def chunk_copy(i):
        slot = i % depth
        return pltpu.make_async_copy(
            src_hbm.at[pl.ds(i * rows, rows), :], stage.at[slot], sem.at[slot])

    for i in range(min(depth - 1, n_chunks)):
        chunk_copy(i).start()
    for i in range(n_chunks):
        if i + depth - 1 < n_chunks:
            chunk_copy(i + depth - 1).start()
        chunk_copy(i).wait()
        dst[i * rows // 2:(i + 1) * rows // 2, :] = _pack_weight(stage[i % depth])


def _shift_rows(ext, k, halo):
    return pltpu.roll(ext, k, axis=0)[halo:]


def _tiles(x_ref):
    n_tiles = x_ref.shape[0] // SEQ_TILE
    return [slice(i * SEQ_TILE, (i + 1) * SEQ_TILE) for i in range(n_tiles)]


def _run_tiles(stages):
    stages[0][0]()
    for ti, (_, body, finish) in enumerate(stages):
        body()
        if ti + 1 < len(stages):
            stages[ti + 1][0]()
        finish()


def _even_kernel(x_ref, mod_ref, g_ref, w_in_ref, conv_ref, lng_ref, lnb_ref,
                 sgw_ref, sgb_ref, w_out_ref, *rest, layer, final_norm, n_cast,
                 own_weights):
    final_g_ref, cast_src, o_ref, cast_dst, scratch = _split_refs(rest, final_norm, n_cast)
    h_scr, y_scr, tail_scr = scratch[:3]
    g, mod, final_g = _layer_rows(mod_ref, g_ref, final_g_ref, layer)
    j = layer // 2
    ts = SEQ_TILE
    n_chunks = ts // CHUNK
    b = pl.program_id(0)
    s = pl.program_id(1)
    if own_weights:
        w_in_hbm, w_out_hbm = w_in_ref, w_out_ref
        w_in_ref, w_out_ref, stage_in, stage_out, sem = scratch[3:]

    @pl.when((b == 0) & (s == 0))
    def _():
        tail_scr[...] = jnp.zeros(tail_scr.shape, _F32)
        if own_weights:
            _fetch_packed(w_in_hbm.at[j], stage_in, sem, w_in_ref)
            _fetch_packed(w_out_hbm.at[j], stage_out, sem, w_out_ref)

    row = lax.broadcasted_iota(jnp.int32, (CHUNK, CHUNK), 0)
    col = lax.broadcasted_iota(jnp.int32, (CHUNK, CHUNK), 1)
    causal = row >= col
    base_b = 4 * E_A
    heads_per_block = COL_BLOCK // HEAD_DIM
    n_blocks = E_B // COL_BLOCK
    bias_cols = []
    for head in range(H_B):
        b_row = jnp.broadcast_to(sgb_ref[head:head + 1, :], (CHUNK, CHUNK))
        bias_cols.append(jnp.concatenate([b_row.T] * n_chunks, axis=1))

    def mix_tile(ti, rows):
        def proj(seg_start, c0, width=COL_BLOCK):
            cols = slice(seg_start + c0, seg_start + c0 + width)
            return _dot(h_scr[rows], _weight(w_in_ref, 0, D_MODEL, cols))

        def a_block(c0):
            cols = slice(c0, c0 + COL_BLOCK)
            a_h = proj(0 * E_A, c0)
            a_c = proj(2 * E_A, c0)
            a_b = proj(1 * E_A, c0)
            a_z = proj(3 * E_A, c0)
            ch = a_c * a_h
            halo = tail_scr[:, cols]
            if ti == 0:
                halo = jnp.where(s > 0, halo, 0.0)
            tail_scr[:, cols] = ch[ts - CONV_HALO:]
            ext = jnp.concatenate([halo, ch], axis=0)
            w = conv_ref[:, cols]
            conv = (_shift_rows(ext, 2, CONV_HALO) * w[0:1]
                    + _shift_rows(ext, 1, CONV_HALO) * w[1:2] + ch * w[2:3])
            y_scr[rows, cols] = (a_b * conv * _silu(a_z)).astype(_BF16)

        def ln_blocks(c0, n_blk):
            b_v = proj(base_b + 1 * E_B, c0, n_blk * COL_BLOCK)
            rhs = []
            for hh in range(n_blk * heads_per_block):
                glanes = slice(c0 + hh * HEAD_DIM, c0 + (hh + 1) * HEAD_DIM)
                v = b_v[:, hh * HEAD_DIM:(hh + 1) * HEAD_DIM]
                dev = v - jnp.mean(v, axis=-1, keepdims=True)
                var = jnp.mean(dev * dev, axis=-1, keepdims=True)
                vn = (dev * lax.rsqrt(var + EPS) * lng_ref[j:j + 1, glanes]
                      + lnb_ref[j:j + 1, glanes])
                vn = vn.astype(_BF16)
                rhs.append(jnp.concatenate(
                    [vn[n * CHUNK:(n + 1) * CHUNK] for n in range(n_chunks)], axis=1))
            return [rhs[i * heads_per_block:(i + 1) * heads_per_block] for i in range(n_blk)]

        def sgu_block(c0, rhs):
            mixed_heads = []
            for hh in range(heads_per_block):
                head = c0 // HEAD_DIM + hh
                w_h = jnp.where(causal, sgw_ref[head], 0.0).astype(_BF16)
                mixed = _dot(w_h, rhs[hh]) + bias_cols[head]
                mixed_heads.append(jnp.concatenate(
                    [mixed[:, n * CHUNK:(n + 1) * CHUNK] for n in range(n_chunks)], axis=0))
            mixed = jnp.concatenate(mixed_heads, axis=1)
            b_u = proj(base_b + 0 * E_B, c0)
            b_z = proj(base_b + 2 * E_B, c0)
            y_scr[rows, E_A + c0:E_A + c0 + COL_BLOCK] = (
                b_u * mixed * _silu(b_z)).astype(_BF16)

        rhs = []

        def head():
            h_scr[rows] = _modulated_input(x_ref[rows], g, mod).astype(_BF16)
            rhs.extend(ln_blocks(0, 2))

        def body():
            for k in range(n_blocks):
                a_block(k * COL_BLOCK)
                if k + 2 < n_blocks:
                    rhs.extend(ln_blocks((k + 2) * COL_BLOCK, 1))
                sgu_block(k * COL_BLOCK, rhs[k])

        def finish():
            _project_out(x_ref, y_scr, w_out_ref, rows, mod, final_g, o_ref)

        return head, body, finish

    _cast_next_weights(cast_src, cast_dst)
    _run_tiles([mix_tile(ti, rows) for ti, rows in enumerate(_tiles(x_ref))])


def _prepare_kernel(c_ref, ada_w_ref, ada_b_ref, wp_ref, pw_ref, ps_ref, *rest, n_mod_units,
                    n_fold_units):
    n_cast = (len(rest) - 2) // 2
    cast_src, cast_dst = rest[:n_cast], rest[n_cast + 2:]
    mod_ref, wmix_ref = rest[n_cast:n_cast + 2]
    step = pl.program_id(0)

    @pl.when(step < n_mod_units)
    def _():
        c_act = _silu(c_ref[...]).astype(_BF16)
        bias = ada_b_ref[pl.ds(step // 3, 1), :]
        mod_ref[...] = _dot(c_act, ada_w_ref[...].astype(_BF16)) + bias

    @pl.when(step < n_fold_units)
    def _():
        def split(v):
            hi = v.astype(_BF16)
            return hi, (v - hi.astype(_F32)).astype(_BF16)

        a_hi, a_lo = split(wp_ref[...])
        b_hi, b_lo = split(pw_ref[...])
        folded = _dot(a_hi, b_hi) + (_dot(a_hi, b_lo) + _dot(a_lo, b_hi))
        scale = ps_ref[pl.ds(step // len(POOL_WINDOWS), 1), :]
        wmix_ref[...] = _pack_weight(folded * scale)

    _cast_next_weights(cast_src, cast_dst)


def _prepare_params(c, ada_w, ada_b, c_w_in, c_pool_w, c_pool_scale, cast_jobs):
    batch = c.shape[0]
    n_odd = c_w_in.shape[0]
    n_groups = len(POOL_WINDOWS)
    n_mod_units, n_fold_units = DEPTH * 3, n_odd * n_groups
    n_steps = PREPARE_STEPS
    assert n_steps >= max(n_mod_units, n_fold_units)

    def mod_unit(i):
        u = jnp.minimum(i, n_mod_units - 1)
        return u // 3, u % 3

    def fold_unit(i):
        u = jnp.minimum(i, n_fold_units - 1)
        return u // n_groups, u % n_groups

    c_in, c_args, c_out, c_shape = _cast_job_specs(cast_jobs, n_steps, lambda i: i)
    return pl.pallas_call(
        functools.partial(_prepare_kernel, n_mod_units=n_mod_units,
                          n_fold_units=n_fold_units),
        grid=(n_steps,),
        in_specs=[
            pl.BlockSpec((batch, D_MODEL), lambda i: (0, 0)),
            pl.BlockSpec((None, D_MODEL, D_MODEL),
                         lambda i: (mod_unit(i)[0], 0, mod_unit(i)[1])),
            pl.BlockSpec((DEPTH, D_MODEL), lambda i: (0, mod_unit(i)[1])),
            pl.BlockSpec((None, D_MODEL, G_C),
                         lambda i: (fold_unit(i)[0], 0, fold_unit(i)[1])),
            pl.BlockSpec((None, None, G_C, G_C), lambda i: (*fold_unit(i), 0, 0)),
            pl.BlockSpec((n_odd, G_C), lambda i: (0, fold_unit(i)[1])),
        ] + c_in,
        out_specs=[
            pl.BlockSpec((None, None, batch, D_MODEL), lambda i: (*mod_unit(i), 0, 0)),
            pl.BlockSpec((None, D_MODEL // 2, G_C),
                         lambda i: (fold_unit(i)[0], 0, fold_unit(i)[1])),
        ] + c_out,
        out_shape=[
            jax.ShapeDtypeStruct((DEPTH, 3, batch, D_MODEL), _F32),
            jax.ShapeDtypeStruct((n_odd, D_MODEL // 2, E_C), jnp.uint32),
        ] + c_shape,
        compiler_params=pltpu.CompilerParams(
            dimension_semantics=("arbitrary",),
            vmem_limit_bytes=VMEM_LIMIT_BYTES),
        name="prepare_params",
    )(c, ada_w, ada_b, c_w_in, c_pool_w, c_pool_scale, *c_args)


def _odd_kernel(x_ref, mod_ref, g_ref, w_mix_ref, w_z_ref, w_out_ref, *rest, layer,
                final_norm, n_cast):
    final_g_ref, cast_src, o_ref, cast_dst, scratch = _split_refs(rest, final_norm, n_cast)
    h_scr, y_scr, tail_scr, lh_scr = scratch
    g, mod, final_g = _layer_rows(mod_ref, g_ref, final_g_ref, layer)
    ts = SEQ_TILE
    b = pl.program_id(0)
    s = pl.program_id(1)
    n_groups = len(POOL_WINDOWS)
    tiles = _tiles(x_ref)

    @pl.when((b == 0) & (s == 0))
    def _():
        tail_scr[...] = jnp.zeros(tail_scr.shape, _F32)

    def window_sums(level, prev):
        sums, new_prev, k = [], [], 1
        for li, win in enumerate(POOL_WINDOWS):
            new_prev.append(level)
            before = level if prev is None else prev[li]
            ext = jnp.concatenate([before[POOL_HALO - 8:], level], axis=0)
            shifted = ext[8 - k:8 - k + POOL_HALO] if k == 8 else _shift_rows(ext, k, 8)
            level = level + shifted
            sums.append(level)
            k *= 2
        return sums, new_prev

    def prepare(ti, rows):
        halo = tail_scr[...]
        if ti == 0:
            halo = jnp.where(s > 0, halo, 0.0)
        _, prev = window_sums(halo, None)
        first_pos = (s * len(tiles) + ti) * ts
        for r0 in range(0, ts, POOL_HALO):
            grp = slice(rows.start + r0, rows.start + r0 + POOL_HALO)
            h = _modulated_input(x_ref[grp], g, mod)
            h_scr[grp] = h.astype(_BF16)
            sums, prev = window_sums(h, prev)
            pos = (first_pos + r0
                   + lax.broadcasted_iota(jnp.int32, (POOL_HALO, HEAD_DIM), 0))
            for gi, win in enumerate(POOL_WINDOWS):
                cnt = jnp.minimum(pos + 1, win).astype(_F32)
                inv_cnt = jnp.concatenate([1.0 / cnt] * (D_MODEL // HEAD_DIM), axis=1)
                lh_scr[gi, grp] = (sums[gi] * inv_cnt - h).astype(_BF16)
        tail_scr[...] = h

    def mix_tile(ti, rows):
        def gate_group(gi):
            gcols = slice(gi * G_C, (gi + 1) * G_C)
            return _silu(_dot(h_scr[rows], _weight(w_z_ref, 0, D_MODEL, gcols)))

        def mix_group(gi, gate):
            gcols = slice(gi * G_C, (gi + 1) * G_C)
            mixed = _dot(lh_scr[gi, rows], _weight(w_mix_ref, 0, D_MODEL, gcols))
            y_scr[rows, gcols] = (mixed * gate).astype(_BF16)

        lead = min(2, n_groups)
        gates = []

        def head():
            prepare(ti, rows)
            gates.extend(gate_group(gi) for gi in range(lead))
            mix_group(0, gates[0])

        def body():
            for gi in range(1, n_groups):
                if gi - 1 + lead < n_groups:
                    gates.append(gate_group(gi - 1 + lead))
                mix_group(gi, gates[gi])

        def finish():
            _project_out(x_ref, y_scr, w_out_ref, rows, mod, final_g, o_ref)

        return head, body, finish

    _cast_next_weights(cast_src, cast_dst)
    _run_tiles([mix_tile(ti, rows) for ti, rows in enumerate(tiles)])


def _cast_job_specs(cast_jobs, n_steps, step_of):
    in_specs, args, out_specs, out_shape = [], [], [], []
    for src, j, col_block, n_cols in cast_jobs:
        n_rows = src.shape[1]
        rows = n_rows // n_steps
        assert rows * n_steps == n_rows and rows % 16 == 0
        in_specs.append(pl.BlockSpec(
            (None, rows, n_cols),
            lambda *g, j=j, cb=col_block: (j, step_of(*g), cb)))
        args.append(src)
        out_specs.append(pl.BlockSpec((rows // 2, n_cols), lambda *g: (step_of(*g), 0)))
        out_shape.append(jax.ShapeDtypeStruct((n_rows // 2, n_cols), jnp.uint32))
    return in_specs, args, out_specs, out_shape


def _layer_call(kernel_fn, name, layer, x, mod, norm_g, weights, weight_specs,
                scratch_shapes, final_g, cast_jobs):
    batch, seq, _ = x.shape
    n_seq = seq // SEQ_BLOCK
    n_steps = batch * n_seq
    x_spec = pl.BlockSpec((None, SEQ_BLOCK, D_MODEL), lambda b, s: (b, s, 0))
    in_specs = [
        x_spec,
        _resident((None, 3, batch, D_MODEL), lambda b, s: (layer, 0, 0, 0)),
        _resident((DEPTH, D_MODEL), lambda b, s: (0, 0)),
    ] + weight_specs
    args = [x, mod, norm_g] + weights
    if final_g is not None:
        in_specs.append(_resident((D_MODEL,), lambda b, s: (0,)))
        args.append(final_g)
    c_in, c_args, c_out, c_shape = _cast_job_specs(
        cast_jobs, n_steps, lambda b, s: b * n_seq + s)
    in_specs += c_in
    args += c_args
    out_specs = [x_spec] + c_out
    out_shape = [jax.ShapeDtypeStruct(x.shape, x.dtype)] + c_shape
    return pl.pallas_call(
        functools.partial(kernel_fn, layer=layer, final_norm=final_g is not None,
                          n_cast=len(cast_jobs)),
        grid=(batch, n_seq),
        in_specs=in_specs,
        out_specs=out_specs,
        out_shape=out_shape,
        scratch_shapes=[
            pltpu.VMEM((SEQ_BLOCK, D_MODEL), _BF16),
            pltpu.VMEM((SEQ_BLOCK, E_INNER), _BF16),
        ] + scratch_shapes,
        compiler_params=pltpu.CompilerParams(
            dimension_semantics=("arbitrary", "arbitrary"),
            vmem_limit_bytes=VMEM_LIMIT_BYTES),
        name=name,
    )(*args)


def _even_layer(layer, x, mod, norm_g, w_in, conv_w, ln_g, ln_b, sgu_w, sgu_b, w_out,
                final_g, cast_jobs, own_weights):
    j = layer // 2
    n_even = conv_w.shape[0]
    const2 = lambda b, s: (0, 0)
    sel2 = lambda b, s: (j, 0, 0)
    sel3 = lambda b, s: (j, 0, 0, 0)
    weights = [w_in, conv_w, ln_g, ln_b, sgu_w, sgu_b, w_out]
    in_hbm = pl.BlockSpec(memory_space=pl.ANY)
    specs = [
        in_hbm if own_weights else _resident((D_MODEL // 2, AB_IN), const2),
        _resident((None, CONV_WIDTH, E_A), sel2),
        _resident((n_even, E_B), const2),
        _resident((n_even, E_B), const2),
        _resident((None, H_B, CHUNK, CHUNK), sel3),
        _resident((None, H_B, CHUNK), sel2),
        in_hbm if own_weights else _resident((E_INNER // 2, D_MODEL), const2),
    ]
    scratch = [pltpu.VMEM((CONV_HALO, E_A), _F32)]
    if own_weights:
        scratch += [
            pltpu.VMEM((D_MODEL // 2, AB_IN), jnp.uint32),
            pltpu.VMEM((E_INNER // 2, D_MODEL), jnp.uint32),
            pltpu.VMEM((FETCH_SLOTS, FETCH_ROWS, AB_IN), _F32),
            pltpu.VMEM((FETCH_SLOTS, 2 * FETCH_ROWS, D_MODEL), _F32),
            pltpu.SemaphoreType.DMA((FETCH_SLOTS,)),
        ]
    return _layer_call(functools.partial(_even_kernel, own_weights=own_weights),
                       "even_layer", layer, x, mod, norm_g, weights, specs,
                       scratch, final_g, cast_jobs)


def _odd_layer(layer, x, mod, norm_g, w_mix, w_z, w_out, final_g, cast_jobs):
    j = layer // 2
    const2 = lambda b, s: (0, 0)
    weights = [w_mix, w_z, w_out]
    specs = [
        _resident((None, D_MODEL // 2, E_C), lambda b, s: (j, 0, 0)),
        _resident((D_MODEL // 2, E_C), const2),
        _resident((E_C // 2, D_MODEL), const2),
    ]
    scratch = [
        pltpu.VMEM((POOL_HALO, D_MODEL), _F32),
        pltpu.VMEM((len(POOL_WINDOWS), SEQ_BLOCK, D_MODEL), _BF16),
    ]
    return _layer_call(_odd_kernel, "odd_layer", layer, x, mod, norm_g, weights, specs,
                       scratch, final_g, cast_jobs)


def kernel(x, c, norm_g, ada_w, ada_b, ab_w_in, ab_conv_w, ab_ln_g, ab_ln_b, ab_sgu_w,
           ab_sgu_b, ab_w_out, c_w_in, c_pool_w, c_pool_scale, c_w_out, final_g):
    batch = x.shape[0]
    assert x.shape[1] % SEQ_BLOCK == 0 and SEQ_BLOCK % SEQ_TILE == 0
    assert SEQ_TILE % CHUNK == 0
    mod, c_w_mix = _prepare_params(c, ada_w, ada_b, c_w_in, c_pool_w, c_pool_scale, [])
    big_w = [ab_w_in, ab_w_out]
    for i in range(DEPTH):
        fg = final_g if i == DEPTH - 1 else None
        nxt = (i + 1) // 2
        if i + 1 == DEPTH:
            cast_jobs = []
        elif i % 2 == 0:
            cast_jobs = [(c_w_in, nxt, 1, E_C), (c_w_out, nxt, 0, D_MODEL)]
        else:
            cast_jobs = [(ab_w_in, nxt, 0, AB_IN), (ab_w_out, nxt, 0, D_MODEL)]
        if i % 2 == 0:
            x, *big_w = _even_layer(i, x, mod, norm_g, big_w[0], ab_conv_w, ab_ln_g,
                                    ab_ln_b, ab_sgu_w, ab_sgu_b, big_w[1], fg, cast_jobs,
                                    own_weights=i == 0)
        else:
            x, *big_w = _odd_layer(i, x, mod, norm_g, c_w_mix, big_w[0], big_w[1], fg,
                                   cast_jobs)
    return x
```

```python
import functools

import jax
import jax.numpy as jnp
from jax import lax
from jax.experimental import pallas as pl
from jax.experimental.pallas import tpu as pltpu

D_MODEL = 1024
DEPTH = 4
E_INNER = 2 * D_MODEL
HEAD_DIM = 128
E_A = E_INNER // 2
E_B = E_INNER - E_A
H_B = E_B // HEAD_DIM
CONV_WIDTH = 3
CHUNK = 128
AB_IN = 4 * E_A + 3 * E_B
E_C = E_INNER
POOL_WINDOWS = (2, 4, 8, 16)
G_C = E_C // len(POOL_WINDOWS)
EPS = 1e-6

SEQ_BLOCK = 1024
SEQ_TILE = 512
COL_BLOCK = 256
CONV_HALO = 8
POOL_HALO = 16
PREPARE_STEPS = 16
V7X_VMEM_BYTES = 64 * 1024 * 1024
VMEM_LIMIT_BYTES = V7X_VMEM_BYTES - 8 * 1024 * 1024

assert all(w == 2 ** (i + 1) for i, w in enumerate(POOL_WINDOWS))
assert POOL_WINDOWS[-1] <= POOL_HALO and CONV_WIDTH - 1 <= CONV_HALO

_F32 = jnp.float32
_BF16 = jnp.bfloat16


_dot = functools.partial(jnp.dot, preferred_element_type=_F32)


def _silu(v):
    return v * jax.nn.sigmoid(v)


def _rmsnorm(x, g):
    return x * lax.rsqrt(jnp.mean(x * x, axis=-1, keepdims=True) + EPS) * g


def _resident(block_shape, index_map):
    return pl.BlockSpec(block_shape, index_map, pipeline_mode=pl.Buffered(1))


def _layer_rows(mod_ref, g_ref, final_g_ref, layer):
    b = pl.program_id(0)
    mod = [mod_ref[k, pl.ds(b, 1), :] for k in range(3)]
    final_g = None if final_g_ref is None else final_g_ref[...].reshape(1, D_MODEL)
    return g_ref[layer:layer + 1, :], mod, final_g


def _modulated_input(x, g, mod):
    return _rmsnorm(x, g) * (1.0 + mod[1]) + mod[0]


def _project_out(x_ref, y_scr, w_out_ref, rows, mod, final_g, o_ref, k_block):
    out = None
    for k0 in range(0, y_scr.shape[1], k_block):
        part = _dot(y_scr[rows, k0:k0 + k_block],
                    _weight(w_out_ref, k0, k0 + k_block, slice(None)))
        out = part if out is None else out + part
    res = x_ref[rows] + mod[2] * out
    if final_g is not None:
        res = _rmsnorm(res, final_g)
    o_ref[rows] = res


def _split_refs(rest, final_norm, n_cast):
    rest = list(rest)
    final_g_ref = rest.pop(0) if final_norm else None
    cast_src = [rest.pop(0) for _ in range(n_cast)]
    o_ref = rest.pop(0)
    cast_dst = [rest.pop(0) for _ in range(n_cast)]
    return final_g_ref, cast_src, o_ref, cast_dst, rest


def _pack_weight(w):
    return pltpu.bitcast(w.astype(_BF16), jnp.uint32)


def _weight(w_ref, k0, k1, cols):
    return pltpu.bitcast(w_ref[k0 // 2:k1 // 2, cols], _BF16)


def _cast_next_weights(cast_src, cast_dst):
    for src, dst in zip(cast_src, cast_dst):
        dst[...] = _pack_weight(src[...])


def _shift_rows(ext, k, halo):
    return pltpu.roll(ext, k, axis=0)[halo:]


def _tiles(x_ref):
    n_tiles = x_ref.shape[0] // SEQ_TILE
    return [slice(i * SEQ_TILE, (i + 1) * SEQ_TILE) for i in range(n_tiles)]


def _run_tiles(stages):
    stages[0][0]()
    for ti, (_, body, finish) in enumerate(stages):
        body()
        if ti + 1 < len(stages):
            stages[ti + 1][0]()
        finish()


def _even_kernel(x_ref, mod_ref, g_ref, w_in_ref, conv_ref, lng_ref, lnb_ref,
                 sgw_ref, sgb_ref, w_out_ref, *rest, layer, final_norm, n_cast):
    final_g_ref, cast_src, o_ref, cast_dst, scratch = _split_refs(rest, final_norm, n_cast)
    h_scr, y_scr, tail_scr = scratch
    g, mod, final_g = _layer_rows(mod_ref, g_ref, final_g_ref, layer)
    j = layer // 2
    ts = SEQ_TILE
    n_chunks = ts // CHUNK
    b = pl.program_id(0)
    s = pl.program_id(1)

    @pl.when((b == 0) & (s == 0))
    def _():
        tail_scr[...] = jnp.zeros(tail_scr.shape, _F32)

    row = lax.broadcasted_iota(jnp.int32, (CHUNK, CHUNK), 0)
    col = lax.broadcasted_iota(jnp.int32, (CHUNK, CHUNK), 1)
    causal = row >= col
    base_b = 4 * E_A
    heads_per_block = COL_BLOCK // HEAD_DIM
    n_blocks = E_B // COL_BLOCK
    bias_cols = []
    for head in range(H_B):
        b_row = jnp.broadcast_to(sgb_ref[head:head + 1, :], (CHUNK, CHUNK))
        bias_cols.append(jnp.concatenate([b_row.T] * n_chunks, axis=1))

    def mix_tile(ti, rows):
        def proj(seg_start, c0, width=COL_BLOCK):
            cols = slice(seg_start + c0, seg_start + c0 + width)
            return _dot(h_scr[rows], _weight(w_in_ref, 0, D_MODEL, cols))

        def a_block(c0):
            cols = slice(c0, c0 + COL_BLOCK)
            a_h = proj(0 * E_A, c0)
            a_c = proj(2 * E_A, c0)
            a_b = proj(1 * E_A, c0)
            a_z = proj(3 * E_A, c0)
            ch = a_c * a_h
            halo = tail_scr[:, cols]
            if ti == 0:
                halo = jnp.where(s > 0, halo, 0.0)
            tail_scr[:, cols] = ch[ts - CONV_HALO:]
            ext = jnp.concatenate([halo, ch], axis=0)
            w = conv_ref[:, cols]
            conv = (_shift_rows(ext, 2, CONV_HALO) * w[0:1]
                    + _shift_rows(ext, 1, CONV_HALO) * w[1:2] + ch * w[2:3])
            y_scr[rows, cols] = (a_b * conv * _silu(a_z)).astype(_BF16)

        def ln_blocks(c0, n_blk):
            b_v = proj(base_b + 1 * E_B, c0, n_blk * COL_BLOCK)
            rhs = []
            for hh in range(n_blk * heads_per_block):
                glanes = slice(c0 + hh * HEAD_DIM, c0 + (hh + 1) * HEAD_DIM)
                v = b_v[:, hh * HEAD_DIM:(hh + 1) * HEAD_DIM]
                dev = v - jnp.mean(v, axis=-1, keepdims=True)
                var = jnp.mean(dev * dev, axis=-1, keepdims=True)
                vn = (dev * lax.rsqrt(var + EPS) * lng_ref[j:j + 1, glanes]
                      + lnb_ref[j:j + 1, glanes])
                vn = vn.astype(_BF16)
                rhs.append(jnp.concatenate(
                    [vn[n * CHUNK:(n + 1) * CHUNK] for n in range(n_chunks)], axis=1))
            return [rhs[i * heads_per_block:(i + 1) * heads_per_block] for i in range(n_blk)]

        def sgu_block(c0, rhs):
            mixed_heads = []
            for hh in range(heads_per_block):
                head = c0 // HEAD_DIM + hh
                w_h = jnp.where(causal, sgw_ref[head], 0.0).astype(_BF16)
                mixed = _dot(w_h, rhs[hh]) + bias_cols[head]
                mixed_heads.append(jnp.concatenate(
                    [mixed[:, n * CHUNK:(n + 1) * CHUNK] for n in range(n_chunks)], axis=0))
            mixed = jnp.concatenate(mixed_heads, axis=1)
            b_u = proj(base_b + 0 * E_B, c0)
            b_z = proj(base_b + 2 * E_B, c0)
            y_scr[rows, E_A + c0:E_A + c0 + COL_BLOCK] = (
                b_u * mixed * _silu(b_z)).astype(_BF16)

        rhs = []

        def head():
            h_scr[rows] = _modulated_input(x_ref[rows], g, mod).astype(_BF16)
            rhs.extend(ln_blocks(0, 2))

        def body():
            for k in range(n_blocks):
                a_block(k * COL_BLOCK)
                if k + 2 < n_blocks:
                    rhs.extend(ln_blocks((k + 2) * COL_BLOCK, 1))
                sgu_block(k * COL_BLOCK, rhs[k])

        def finish():
            _project_out(x_ref, y_scr, w_out_ref, rows, mod, final_g, o_ref, COL_BLOCK)

        return head, body, finish

    _cast_next_weights(cast_src, cast_dst)
    _run_tiles([mix_tile(ti, rows) for ti, rows in enumerate(_tiles(x_ref))])


def _prepare_kernel(c_ref, ada_w_ref, ada_b_ref, wp_ref, pw_ref, ps_ref, *rest, n_mod_units,
                    n_fold_units):
    n_cast = (len(rest) - 2) // 2
    cast_src, cast_dst = rest[:n_cast], rest[n_cast + 2:]
    mod_ref, wmix_ref = rest[n_cast:n_cast + 2]
    step = pl.program_id(0)

    @pl.when(step < n_mod_units)
    def _():
        c_act = _silu(c_ref[...]).astype(_BF16)
        bias = ada_b_ref[pl.ds(step // 3, 1), :]
        mod_ref[...] = _dot(c_act, ada_w_ref[...].astype(_BF16)) + bias

    @pl.when(step < n_fold_units)
    def _():
        def split(v):
            hi = v.astype(_BF16)
            return hi, (v - hi.astype(_F32)).astype(_BF16)

        a_hi, a_lo = split(wp_ref[...])
        b_hi, b_lo = split(pw_ref[...])
        folded = _dot(a_hi, b_hi) + (_dot(a_hi, b_lo) + _dot(a_lo, b_hi))
        scale = ps_ref[pl.ds(step // len(POOL_WINDOWS), 1), :]
        wmix_ref[...] = _pack_weight(folded * scale)

    _cast_next_weights(cast_src, cast_dst)


def _prepare_params(c, ada_w, ada_b, c_w_in, c_pool_w, c_pool_scale, cast_jobs):
    batch = c.shape[0]
    n_odd = c_w_in.shape[0]
    n_groups = len(POOL_WINDOWS)
    n_mod_units, n_fold_units = DEPTH * 3, n_odd * n_groups
    n_steps = PREPARE_STEPS
    assert n_steps >= max(n_mod_units, n_fold_units)

    def mod_unit(i):
        u = jnp.minimum(i, n_mod_units - 1)
        return u // 3, u % 3

    def fold_unit(i):
        u = jnp.minimum(i, n_fold_units - 1)
        return u // n_groups, u % n_groups

    c_in, c_args, c_out, c_shape = _cast_job_specs(cast_jobs, n_steps, lambda i: i)
    return pl.pallas_call(
        functools.partial(_prepare_kernel, n_mod_units=n_mod_units,
                          n_fold_units=n_fold_units),
        grid=(n_steps,),
        in_specs=[
            pl.BlockSpec((batch, D_MODEL), lambda i: (0, 0)),
            pl.BlockSpec((None, D_MODEL, D_MODEL),
                         lambda i: (mod_unit(i)[0], 0, mod_unit(i)[1])),
            pl.BlockSpec((DEPTH, D_MODEL), lambda i: (0, mod_unit(i)[1])),
            pl.BlockSpec((None, D_MODEL, G_C),
                         lambda i: (fold_unit(i)[0], 0, fold_unit(i)[1])),
            pl.BlockSpec((None, None, G_C, G_C), lambda i: (*fold_unit(i), 0, 0)),
            pl.BlockSpec((n_odd, G_C), lambda i: (0, fold_unit(i)[1])),
        ] + c_in,
        out_specs=[
            pl.BlockSpec((None, None, batch, D_MODEL), lambda i: (*mod_unit(i), 0, 0)),
            pl.BlockSpec((None, D_MODEL // 2, G_C),
                         lambda i: (fold_unit(i)[0], 0, fold_unit(i)[1])),
        ] + c_out,
        out_shape=[
            jax.ShapeDtypeStruct((DEPTH, 3, batch, D_MODEL), _F32),
            jax.ShapeDtypeStruct((n_odd, D_MODEL // 2, E_C), jnp.uint32),
        ] + c_shape,
        compiler_params=pltpu.CompilerParams(
            dimension_semantics=("arbitrary",),
            vmem_limit_bytes=VMEM_LIMIT_BYTES),
        name="prepare_params",
    )(c, ada_w, ada_b, c_w_in, c_pool_w, c_pool_scale, *c_args)


def _odd_kernel(x_ref, mod_ref, g_ref, w_mix_ref, w_z_ref, w_out_ref, *rest, layer,
                final_norm, n_cast):
    final_g_ref, cast_src, o_ref, cast_dst, scratch = _split_refs(rest, final_norm, n_cast)
    h_scr, y_scr, tail_scr, lh_scr = scratch
    g, mod, final_g = _layer_rows(mod_ref, g_ref, final_g_ref, layer)
    ts = SEQ_TILE
    b = pl.program_id(0)
    s = pl.program_id(1)
    n_groups = len(POOL_WINDOWS)
    tiles = _tiles(x_ref)

    @pl.when((b == 0) & (s == 0))
    def _():
        tail_scr[...] = jnp.zeros(tail_scr.shape, _F32)

    def window_sums(level, prev):
        sums, new_prev, k = [], [], 1
        for li, win in enumerate(POOL_WINDOWS):
            new_prev.append(level)
            before = level if prev is None else prev[li]
            ext = jnp.concatenate([before[POOL_HALO - 8:], level], axis=0)
            shifted = ext[8 - k:8 - k + POOL_HALO] if k == 8 else _shift_rows(ext, k, 8)
            level = level + shifted
            sums.append(level)
            k *= 2
        return sums, new_prev

    def prepare(ti, rows):
        halo = tail_scr[...]
        if ti == 0:
            halo = jnp.where(s > 0, halo, 0.0)
        _, prev = window_sums(halo, None)
        first_pos = (s * len(tiles) + ti) * ts
        for r0 in range(0, ts, POOL_HALO):
            grp = slice(rows.start + r0, rows.start + r0 + POOL_HALO)
            h = _modulated_input(x_ref[grp], g, mod)
            h_scr[grp] = h.astype(_BF16)
            sums, prev = window_sums(h, prev)
            pos = (first_pos + r0
                   + lax.broadcasted_iota(jnp.int32, (POOL_HALO, HEAD_DIM), 0))
            for gi, win in enumerate(POOL_WINDOWS):
                cnt = jnp.minimum(pos + 1, win).astype(_F32)
                inv_cnt = jnp.concatenate([1.0 / cnt] * (D_MODEL // HEAD_DIM), axis=1)
                lh_scr[gi, grp] = (sums[gi] * inv_cnt - h).astype(_BF16)
        tail_scr[...] = h

    def mix_tile(ti, rows):
        def gate_group(gi):
            gcols = slice(gi * G_C, (gi + 1) * G_C)
            return _silu(_dot(h_scr[rows], _weight(w_z_ref, 0, D_MODEL, gcols)))

        def mix_group(gi, gate):
            gcols = slice(gi * G_C, (gi + 1) * G_C)
            mixed = _dot(lh_scr[gi, rows], _weight(w_mix_ref, 0, D_MODEL, gcols))
            y_scr[rows, gcols] = (mixed * gate).astype(_BF16)

        lead = min(2, n_groups)
        gates = []

        def head():
            prepare(ti, rows)
            gates.extend(gate_group(gi) for gi in range(lead))
            mix_group(0, gates[0])

        def body():
            for gi in range(1, n_groups):
                if gi - 1 + lead < n_groups:
                    gates.append(gate_group(gi - 1 + lead))
                mix_group(gi, gates[gi])

        def finish():
            _project_out(x_ref, y_scr, w_out_ref, rows, mod, final_g, o_ref, G_C)

        return head, body, finish

    _cast_next_weights(cast_src, cast_dst)
    _run_tiles([mix_tile(ti, rows) for ti, rows in enumerate(tiles)])


def _cast_job_specs(cast_jobs, n_steps, step_of):
    in_specs, args, out_specs, out_shape = [], [], [], []
    for src, j, col_block, n_cols in cast_jobs:
        n_rows = src.shape[1]
        rows = n_rows // n_steps
        assert rows * n_steps == n_rows and rows % 16 == 0
        in_specs.append(pl.BlockSpec(
            (None, rows, n_cols),
            lambda *g, j=j, cb=col_block: (j, step_of(*g), cb)))
        args.append(src)
        out_specs.append(pl.BlockSpec((rows // 2, n_cols), lambda *g: (step_of(*g), 0)))
        out_shape.append(jax.ShapeDtypeStruct((n_rows // 2, n_cols), jnp.uint32))
    return in_specs, args, out_specs, out_shape


def _layer_call(kernel_fn, name, layer, x, mod, norm_g, weights, weight_specs,
                scratch_shapes, final_g, cast_jobs):
    batch, seq, _ = x.shape
    n_seq = seq // SEQ_BLOCK
    n_steps = batch * n_seq
    x_spec = pl.BlockSpec((None, SEQ_BLOCK, D_MODEL), lambda b, s: (b, s, 0))
    in_specs = [
        x_spec,
        _resident((None, 3, batch, D_MODEL), lambda b, s: (layer, 0, 0, 0)),
        _resident((DEPTH, D_MODEL), lambda b, s: (0, 0)),
    ] + weight_specs
    args = [x, mod, norm_g] + weights
    if final_g is not None:
        in_specs.append(_resident((D_MODEL,), lambda b, s: (0,)))
        args.append(final_g)
    c_in, c_args, c_out, c_shape = _cast_job_specs(
        cast_jobs, n_steps, lambda b, s: b * n_seq + s)
    in_specs += c_in
    args += c_args
    out_specs = [x_spec] + c_out
    out_shape = [jax.ShapeDtypeStruct(x.shape, x.dtype)] + c_shape
    return pl.pallas_call(
        functools.partial(kernel_fn, layer=layer, final_norm=final_g is not None,
                          n_cast=len(cast_jobs)),
        grid=(batch, n_seq),
        in_specs=in_specs,
        out_specs=out_specs,
        out_shape=out_shape,
        scratch_shapes=[
            pltpu.VMEM((SEQ_BLOCK, D_MODEL), _BF16),
            pltpu.VMEM((SEQ_BLOCK, E_INNER), _BF16),
        ] + scratch_shapes,
        compiler_params=pltpu.CompilerParams(
            dimension_semantics=("arbitrary", "arbitrary"),
            vmem_limit_bytes=VMEM_LIMIT_BYTES),
        name=name,
    )(*args)


def _even_layer(layer, x, mod, norm_g, w_in, conv_w, ln_g, ln_b, sgu_w, sgu_b, w_out,
                final_g, cast_jobs):
    j = layer // 2
    n_even = conv_w.shape[0]
    const2 = lambda b, s: (0, 0)
    sel2 = lambda b, s: (j, 0, 0)
    sel3 = lambda b, s: (j, 0, 0, 0)
    weights = [w_in, conv_w, ln_g, ln_b, sgu_w, sgu_b, w_out]
    specs = [
        _resident((D_MODEL // 2, AB_IN), const2),
        _resident((None, CONV_WIDTH, E_A), sel2),
        _resident((n_even, E_B), const2),
        _resident((n_even, E_B), const2),
        _resident((None, H_B, CHUNK, CHUNK), sel3),
        _resident((None, H_B, CHUNK), sel2),
        _resident((E_INNER // 2, D_MODEL), const2),
    ]
    scratch = [pltpu.VMEM((CONV_HALO, E_A), _F32)]
    return _layer_call(_even_kernel, "even_layer", layer, x, mod, norm_g, weights, specs,
                       scratch, final_g, cast_jobs)


def _odd_layer(layer, x, mod, norm_g, w_mix, w_z, w_out, final_g, cast_jobs):
    j = layer // 2
    const2 = lambda b, s: (0, 0)
    weights = [w_mix, w_z, w_out]
    specs = [
        _resident((None, D_MODEL // 2, E_C), lambda b, s: (j, 0, 0)),
        _resident((D_MODEL // 2, E_C), const2),
        _resident((E_C // 2, D_MODEL), const2),
    ]
    scratch = [
        pltpu.VMEM((POOL_HALO, D_MODEL), _F32),
        pltpu.VMEM((len(POOL_WINDOWS), SEQ_BLOCK, D_MODEL), _BF16),
    ]
    return _layer_call(_odd_kernel, "odd_layer", layer, x, mod, norm_g, weights, specs,
                       scratch, final_g, cast_jobs)


def kernel(x, c, norm_g, ada_w, ada_b, ab_w_in, ab_conv_w, ab_ln_g, ab_ln_b, ab_sgu_w,
           ab_sgu_b, ab_w_out, c_w_in, c_pool_w, c_pool_scale, c_w_out, final_g):
    batch = x.shape[0]
    assert x.shape[1] % SEQ_BLOCK == 0 and SEQ_BLOCK % SEQ_TILE == 0
    assert SEQ_TILE % CHUNK == 0
    mod, c_w_mix, *big_w = _prepare_params(
        c, ada_w, ada_b, c_w_in, c_pool_w, c_pool_scale,
        [(ab_w_in, 0, 0, AB_IN), (ab_w_out, 0, 0, D_MODEL)])
    for i in range(DEPTH):
        fg = final_g if i == DEPTH - 1 else None
        nxt = (i + 1) // 2
        if i + 1 == DEPTH:
            cast_jobs = []
        elif i % 2 == 0:
            cast_jobs = [(c_w_in, nxt, 1, E_C), (c_w_out, nxt, 0, D_MODEL)]
        else:
            cast_jobs = [(ab_w_in, nxt, 0, AB_IN), (ab_w_out, nxt, 0, D_MODEL)]
        if i % 2 == 0:
            x, *big_w = _even_layer(i, x, mod, norm_g, big_w[0], ab_conv_w, ab_ln_g,
                                    ab_ln_b, ab_sgu_w, ab_sgu_b, big_w[1], fg, cast_jobs)
        else:
            x, *big_w = _odd_layer(i, x, mod, norm_g, c_w_mix, big_w[0], big_w[1], fg,
                                   cast_jobs)
    return x
```

```python
import functools

import jax
import jax.numpy as jnp
from jax import lax
from jax.experimental import pallas as pl
from jax.experimental.pallas import tpu as pltpu

D_MODEL = 1024
DEPTH = 4
E_INNER = 2 * D_MODEL
HEAD_DIM = 128
E_A = E_INNER // 2
E_B = E_INNER - E_A
H_B = E_B // HEAD_DIM
CONV_WIDTH = 3
CHUNK = 128
AB_IN = 4 * E_A + 3 * E_B
E_C = E_INNER
POOL_WINDOWS = (2, 4, 8, 16)
G_C = E_C // len(POOL_WINDOWS)
EPS = 1e-6

SEQ_BLOCK = 1024
SEQ_TILE = 512
COL_BLOCK = 256
CONV_HALO = 8
POOL_HALO = 16
PREPARE_STEPS = 16
V7X_VMEM_BYTES = 64 * 1024 * 1024
VMEM_LIMIT_BYTES = V7X_VMEM_BYTES - 8 * 1024 * 1024

assert all(w == 2 ** (i + 1) for i, w in enumerate(POOL_WINDOWS))
assert POOL_WINDOWS[-1] <= POOL_HALO and CONV_WIDTH - 1 <= CONV_HALO

_F32 = jnp.float32
_BF16 = jnp.bfloat16


_dot = functools.partial(jnp.dot, preferred_element_type=_F32)


def _silu(v):
    return v * jax.nn.sigmoid(v)


def _rmsnorm(x, g):
    return x * lax.rsqrt(jnp.mean(x * x, axis=-1, keepdims=True) + EPS) * g


def _resident(block_shape, index_map):
    return pl.BlockSpec(block_shape, index_map, pipeline_mode=pl.Buffered(1))


def _layer_rows(mod_ref, g_ref, final_g_ref, layer):
    b = pl.program_id(0)
    mod = [mod_ref[k, pl.ds(b, 1), :] for k in range(3)]
    final_g = None if final_g_ref is None else final_g_ref[...].reshape(1, D_MODEL)
    g = g_ref[layer:layer + 1, :]
    mod.append(g * (1.0 + mod[1]))
    return g, mod, final_g


def _modulated_input(x, g, mod):
    del g
    return x * lax.rsqrt(jnp.mean(x * x, axis=-1, keepdims=True) + EPS) * mod[3] + mod[0]


def _project_out(x_ref, y_scr, w_out_ref, rows, mod, final_g, o_ref, k_block):
    out = None
    for k0 in range(0, y_scr.shape[1], k_block):
        part = _dot(y_scr[rows, k0:k0 + k_block],
                    _weight(w_out_ref, k0, k0 + k_block, slice(None)))
        out = part if out is None else out + part
    res = x_ref[rows] + mod[2] * out
    if final_g is not None:
        res = _rmsnorm(res, final_g)
    o_ref[rows] = res


def _split_refs(rest, final_norm, n_cast):
    rest = list(rest)
    final_g_ref = rest.pop(0) if final_norm else None
    cast_src = [rest.pop(0) for _ in range(n_cast)]
    o_ref = rest.pop(0)
    cast_dst = [rest.pop(0) for _ in range(n_cast)]
    return final_g_ref, cast_src, o_ref, cast_dst, rest


def _pack_weight(w):
    return pltpu.bitcast(w.astype(_BF16), jnp.uint32)


def _weight(w_ref, k0, k1, cols):
    return pltpu.bitcast(w_ref[k0 // 2:k1 // 2, cols], _BF16)


def _cast_next_weights(cast_src, cast_dst):
    for src, dst in zip(cast_src, cast_dst):
        dst[...] = _pack_weight(src[...])


def _shift_rows(ext, k, halo):
    return pltpu.roll(ext, k, axis=0)[halo:]


def _tiles(x_ref):
    n_tiles = x_ref.shape[0] // SEQ_TILE
    return [slice(i * SEQ_TILE, (i + 1) * SEQ_TILE) for i in range(n_tiles)]


def _run_tiles(stages):
    stages[0][0]()
    for ti, (_, body, finish) in enumerate(stages):
        body()
        if ti + 1 < len(stages):
            stages[ti + 1][0]()
        finish()


def _even_kernel(x_ref, mod_ref, g_ref, w_in_ref, conv_ref, lng_ref, lnb_ref,
                 sgw_ref, sgb_ref, w_out_ref, *rest, layer, final_norm, n_cast):
    final_g_ref, cast_src, o_ref, cast_dst, scratch = _split_refs(rest, final_norm, n_cast)
    h_scr, y_scr, tail_scr = scratch
    g, mod, final_g = _layer_rows(mod_ref, g_ref, final_g_ref, layer)
    j = layer // 2
    ts = SEQ_TILE
    n_chunks = ts // CHUNK
    b = pl.program_id(0)
    s = pl.program_id(1)

    @pl.when((b == 0) & (s == 0))
    def _():
        tail_scr[...] = jnp.zeros(tail_scr.shape, _F32)

    row = lax.broadcasted_iota(jnp.int32, (CHUNK, CHUNK), 0)
    col = lax.broadcasted_iota(jnp.int32, (CHUNK, CHUNK), 1)
    causal = row >= col
    base_b = 4 * E_A
    heads_per_block = COL_BLOCK // HEAD_DIM
    n_blocks = E_B // COL_BLOCK
    bias_cols = []
    for head in range(H_B):
        b_row = jnp.broadcast_to(sgb_ref[head:head + 1, :], (CHUNK, CHUNK))
        bias_cols.append(jnp.concatenate([b_row.T] * n_chunks, axis=1))

    def mix_tile(ti, rows):
        def proj(seg_start, c0, width=COL_BLOCK):
            cols = slice(seg_start + c0, seg_start + c0 + width)
            return _dot(h_scr[rows], _weight(w_in_ref, 0, D_MODEL, cols))

        def a_block(c0):
            cols = slice(c0, c0 + COL_BLOCK)
            a_h = proj(0 * E_A, c0)
            a_c = proj(2 * E_A, c0)
            a_b = proj(1 * E_A, c0)
            a_z = proj(3 * E_A, c0)
            ch = a_c * a_h
            halo = tail_scr[:, cols]
            if ti == 0:
                halo = jnp.where(s > 0, halo, 0.0)
            tail_scr[:, cols] = ch[ts - CONV_HALO:]
            ext = jnp.concatenate([halo, ch], axis=0)
            w = conv_ref[:, cols]
            conv = (_shift_rows(ext, 2, CONV_HALO) * w[0:1]
                    + _shift_rows(ext, 1, CONV_HALO) * w[1:2] + ch * w[2:3])
            y_scr[rows, cols] = (a_b * conv * _silu(a_z)).astype(_BF16)

        def ln_blocks(c0, n_blk):
            b_v = proj(base_b + 1 * E_B, c0, n_blk * COL_BLOCK)
            rhs = []
            for hh in range(n_blk * heads_per_block):
                glanes = slice(c0 + hh * HEAD_DIM, c0 + (hh + 1) * HEAD_DIM)
                v = b_v[:, hh * HEAD_DIM:(hh + 1) * HEAD_DIM]
                dev = v - jnp.mean(v, axis=-1, keepdims=True)
                var = jnp.mean(dev * dev, axis=-1, keepdims=True)
                vn = (dev * lax.rsqrt(var + EPS) * lng_ref[j:j + 1, glanes]
                      + lnb_ref[j:j + 1, glanes])
                vn = vn.astype(_BF16)
                rhs.append(jnp.concatenate(
                    [vn[n * CHUNK:(n + 1) * CHUNK] for n in range(n_chunks)], axis=1))
            return [rhs[i * heads_per_block:(i + 1) * heads_per_block] for i in range(n_blk)]

        def sgu_block(c0, rhs):
            mixed_heads = []
            for hh in range(heads_per_block):
                head = c0 // HEAD_DIM + hh
                w_h = jnp.where(causal, sgw_ref[head], 0.0).astype(_BF16)
                mixed = _dot(w_h, rhs[hh]) + bias_cols[head]
                mixed_heads.append(jnp.concatenate(
                    [mixed[:, n * CHUNK:(n + 1) * CHUNK] for n in range(n_chunks)], axis=0))
            mixed = jnp.concatenate(mixed_heads, axis=1)
            b_u = proj(base_b + 0 * E_B, c0)
            b_z = proj(base_b + 2 * E_B, c0)
            y_scr[rows, E_A + c0:E_A + c0 + COL_BLOCK] = (
                b_u * mixed * _silu(b_z)).astype(_BF16)

        rhs = []

        def head():
            h_scr[rows] = _modulated_input(x_ref[rows], g, mod).astype(_BF16)
            rhs.extend(ln_blocks(0, 2))

        def body():
            for k in range(n_blocks):
                a_block(k * COL_BLOCK)
                if k + 2 < n_blocks:
                    rhs.extend(ln_blocks((k + 2) * COL_BLOCK, 1))
                sgu_block(k * COL_BLOCK, rhs[k])

        def finish():
            _project_out(x_ref, y_scr, w_out_ref, rows, mod, final_g, o_ref, COL_BLOCK)

        return head, body, finish

    _cast_next_weights(cast_src, cast_dst)
    _run_tiles([mix_tile(ti, rows) for ti, rows in enumerate(_tiles(x_ref))])


def _prepare_kernel(c_ref, ada_w_ref, ada_b_ref, wp_ref, pw_ref, ps_ref, *rest, n_mod_units,
                    n_fold_units):
    n_cast = (len(rest) - 2) // 2
    cast_src, cast_dst = rest[:n_cast], rest[n_cast + 2:]
    mod_ref, wmix_ref = rest[n_cast:n_cast + 2]
    step = pl.program_id(0)

    @pl.when(step < n_mod_units)
    def _():
        c_act = _silu(c_ref[...]).astype(_BF16)
        bias = ada_b_ref[pl.ds(step // 3, 1), :]
        mod_ref[...] = _dot(c_act, ada_w_ref[...].astype(_BF16)) + bias

    @pl.when(step < n_fold_units)
    def _():
        def split(v):
            hi = v.astype(_BF16)
            return hi, (v - hi.astype(_F32)).astype(_BF16)

        a_hi, a_lo = split(wp_ref[...])
        b_hi, b_lo = split(pw_ref[...])
        folded = _dot(a_hi, b_hi) + (_dot(a_hi, b_lo) + _dot(a_lo, b_hi))
        scale = ps_ref[pl.ds(step // len(POOL_WINDOWS), 1), :]
        wmix_ref[...] = _pack_weight(folded * scale)

    _cast_next_weights(cast_src, cast_dst)


def _prepare_params(c, ada_w, ada_b, c_w_in, c_pool_w, c_pool_scale, cast_jobs):
    batch = c.shape[0]
    n_odd = c_w_in.shape[0]
    n_groups = len(POOL_WINDOWS)
    n_mod_units, n_fold_units = DEPTH * 3, n_odd * n_groups
    n_steps = PREPARE_STEPS
    assert n_steps >= max(n_mod_units, n_fold_units)

    def mod_unit(i):
        u = jnp.minimum(i, n_mod_units - 1)
        return u // 3, u % 3

    def fold_unit(i):
        u = jnp.minimum(i, n_fold_units - 1)
        return u // n_groups, u % n_groups

    c_in, c_args, c_out, c_shape = _cast_job_specs(cast_jobs, n_steps, lambda i: i)
    return pl.pallas_call(
        functools.partial(_prepare_kernel, n_mod_units=n_mod_units,
                          n_fold_units=n_fold_units),
        grid=(n_steps,),
        in_specs=[
            pl.BlockSpec((batch, D_MODEL), lambda i: (0, 0)),
            pl.BlockSpec((None, D_MODEL, D_MODEL),
                         lambda i: (mod_unit(i)[0], 0, mod_unit(i)[1])),
            pl.BlockSpec((DEPTH, D_MODEL), lambda i: (0, mod_unit(i)[1])),
            pl.BlockSpec((None, D_MODEL, G_C),
                         lambda i: (fold_unit(i)[0], 0, fold_unit(i)[1])),
            pl.BlockSpec((None, None, G_C, G_C), lambda i: (*fold_unit(i), 0, 0)),
            pl.BlockSpec((n_odd, G_C), lambda i: (0, fold_unit(i)[1])),
        ] + c_in,
        out_specs=[
            pl.BlockSpec((None, None, batch, D_MODEL), lambda i: (*mod_unit(i), 0, 0)),
            pl.BlockSpec((None, D_MODEL // 2, G_C),
                         lambda i: (fold_unit(i)[0], 0, fold_unit(i)[1])),
        ] + c_out,
        out_shape=[
            jax.ShapeDtypeStruct((DEPTH, 3, batch, D_MODEL), _F32),
            jax.ShapeDtypeStruct((n_odd, D_MODEL // 2, E_C), jnp.uint32),
        ] + c_shape,
        compiler_params=pltpu.CompilerParams(
            dimension_semantics=("arbitrary",),
            vmem_limit_bytes=VMEM_LIMIT_BYTES),
        name="prepare_params",
    )(c, ada_w, ada_b, c_w_in, c_pool_w, c_pool_scale, *c_args)


def _odd_kernel(x_ref, mod_ref, g_ref, w_mix_ref, w_z_ref, w_out_ref, *rest, layer,
                final_norm, n_cast):
    final_g_ref, cast_src, o_ref, cast_dst, scratch = _split_refs(rest, final_norm, n_cast)
    h_scr, y_scr, tail_scr, lh_scr = scratch
    g, mod, final_g = _layer_rows(mod_ref, g_ref, final_g_ref, layer)
    ts = SEQ_TILE
    b = pl.program_id(0)
    s = pl.program_id(1)
    n_groups = len(POOL_WINDOWS)
    tiles = _tiles(x_ref)

    @pl.when((b == 0) & (s == 0))
    def _():
        tail_scr[...] = jnp.zeros(tail_scr.shape, _F32)

    def window_sums(level, prev):
        sums, new_prev, k = [], [], 1
        for li, win in enumerate(POOL_WINDOWS):
            new_prev.append(level)
            before = level if prev is None else prev[li]
            ext = jnp.concatenate([before[POOL_HALO - 8:], level], axis=0)
            shifted = ext[8 - k:8 - k + POOL_HALO] if k == 8 else _shift_rows(ext, k, 8)
            level = level + shifted
            sums.append(level)
            k *= 2
        return sums, new_prev

    def prepare(ti, rows):
        halo = tail_scr[...]
        if ti == 0:
            halo = jnp.where(s > 0, halo, 0.0)
        _, prev = window_sums(halo, None)
        first_pos = (s * len(tiles) + ti) * ts
        for r0 in range(0, ts, POOL_HALO):
            grp = slice(rows.start + r0, rows.start + r0 + POOL_HALO)
            h = _modulated_input(x_ref[grp], g, mod)
            h_scr[grp] = h.astype(_BF16)
            sums, prev = window_sums(h, prev)
            pos = (first_pos + r0
                   + lax.broadcasted_iota(jnp.int32, (POOL_HALO, HEAD_DIM), 0))
            for gi, win in enumerate(POOL_WINDOWS):
                cnt = jnp.minimum(pos + 1, win).astype(_F32)
                inv_cnt = jnp.concatenate([1.0 / cnt] * (D_MODEL // HEAD_DIM), axis=1)
                lh_scr[gi, grp] = (sums[gi] * inv_cnt - h).astype(_BF16)
        tail_scr[...] = h

    def mix_tile(ti, rows):
        def gate_group(gi):
            gcols = slice(gi * G_C, (gi + 1) * G_C)
            return _silu(_dot(h_scr[rows], _weight(w_z_ref, 0, D_MODEL, gcols)))

        def mix_group(gi, gate):
            gcols = slice(gi * G_C, (gi + 1) * G_C)
            mixed = _dot(lh_scr[gi, rows], _weight(w_mix_ref, 0, D_MODEL, gcols))
            y_scr[rows, gcols] = (mixed * gate).astype(_BF16)

        lead = min(2, n_groups)
        gates = []

        def head():
            prepare(ti, rows)
            gates.extend(gate_group(gi) for gi in range(lead))
            mix_group(0, gates[0])

        def body():
            for gi in range(1, n_groups):
                if gi - 1 + lead < n_groups:
                    gates.append(gate_group(gi - 1 + lead))
                mix_group(gi, gates[gi])

        def finish():
            _project_out(x_ref, y_scr, w_out_ref, rows, mod, final_g, o_ref, G_C)

        return head, body, finish

    _cast_next_weights(cast_src, cast_dst)
    _run_tiles([mix_tile(ti, rows) for ti, rows in enumerate(tiles)])


def _cast_job_specs(cast_jobs, n_steps, step_of):
    in_specs, args, out_specs, out_shape = [], [], [], []
    for src, j, col_block, n_cols in cast_jobs:
        n_rows = src.shape[1]
        rows = n_rows // n_steps
        assert rows * n_steps == n_rows and rows % 16 == 0
        in_specs.append(pl.BlockSpec(
            (None, rows, n_cols),
            lambda *g, j=j, cb=col_block: (j, step_of(*g), cb)))
        args.append(src)
        out_specs.append(pl.BlockSpec((rows // 2, n_cols), lambda *g: (step_of(*g), 0)))
        out_shape.append(jax.ShapeDtypeStruct((n_rows // 2, n_cols), jnp.uint32))
    return in_specs, args, out_specs, out_shape


def _layer_call(kernel_fn, name, layer, x, mod, norm_g, weights, weight_specs,
                scratch_shapes, final_g, cast_jobs):
    batch, seq, _ = x.shape
    n_seq = seq // SEQ_BLOCK
    n_steps = batch * n_seq
    x_spec = pl.BlockSpec((None, SEQ_BLOCK, D_MODEL), lambda b, s: (b, s, 0))
    in_specs = [
        x_spec,
        _resident((None, 3, batch, D_MODEL), lambda b, s: (layer, 0, 0, 0)),
        _resident((DEPTH, D_MODEL), lambda b, s: (0, 0)),
    ] + weight_specs
    args = [x, mod, norm_g] + weights
    if final_g is not None:
        in_specs.append(_resident((D_MODEL,), lambda b, s: (0,)))
        args.append(final_g)
    c_in, c_args, c_out, c_shape = _cast_job_specs(
        cast_jobs, n_steps, lambda b, s: b * n_seq + s)
    in_specs += c_in
    args += c_args
    out_specs = [x_spec] + c_out
    out_shape = [jax.ShapeDtypeStruct(x.shape, x.dtype)] + c_shape
    return pl.pallas_call(
        functools.partial(kernel_fn, layer=layer, final_norm=final_g is not None,
                          n_cast=len(cast_jobs)),
        grid=(batch, n_seq),
        in_specs=in_specs,
        out_specs=out_specs,
        out_shape=out_shape,
        scratch_shapes=[
            pltpu.VMEM((SEQ_BLOCK, D_MODEL), _BF16),
            pltpu.VMEM((SEQ_BLOCK, E_INNER), _BF16),
        ] + scratch_shapes,
        compiler_params=pltpu.CompilerParams(
            dimension_semantics=("arbitrary", "arbitrary"),
            vmem_limit_bytes=VMEM_LIMIT_BYTES),
        name=name,
    )(*args)


def _even_layer(layer, x, mod, norm_g, w_in, conv_w, ln_g, ln_b, sgu_w, sgu_b, w_out,
                final_g, cast_jobs):
    j = layer // 2
    n_even = conv_w.shape[0]
    const2 = lambda b, s: (0, 0)
    sel2 = lambda b, s: (j, 0, 0)
    sel3 = lambda b, s: (j, 0, 0, 0)
    weights = [w_in, conv_w, ln_g, ln_b, sgu_w, sgu_b, w_out]
    specs = [
        _resident((D_MODEL // 2, AB_IN), const2),
        _resident((None, CONV_WIDTH, E_A), sel2),
        _resident((n_even, E_B), const2),
        _resident((n_even, E_B), const2),
        _resident((None, H_B, CHUNK, CHUNK), sel3),
        _resident((None, H_B, CHUNK), sel2),
        _resident((E_INNER // 2, D_MODEL), const2),
    ]
    scratch = [pltpu.VMEM((CONV_HALO, E_A), _F32)]
    return _layer_call(_even_kernel, "even_layer", layer, x, mod, norm_g, weights, specs,
                       scratch, final_g, cast_jobs)


def _odd_layer(layer, x, mod, norm_g, w_mix, w_z, w_out, final_g, cast_jobs):
    j = layer // 2
    const2 = lambda b, s: (0, 0)
    weights = [w_mix, w_z, w_out]
    specs = [
        _resident((None, D_MODEL // 2, E_C), lambda b, s: (j, 0, 0)),
        _resident((D_MODEL // 2, E_C), const2),
        _resident((E_C // 2, D_MODEL), const2),
    ]
    scratch = [
        pltpu.VMEM((POOL_HALO, D_MODEL), _F32),
        pltpu.VMEM((len(POOL_WINDOWS), SEQ_BLOCK, D_MODEL), _BF16),
    ]
    return _layer_call(_odd_kernel, "odd_layer", layer, x, mod, norm_g, weights, specs,
                       scratch, final_g, cast_jobs)


def kernel(x, c, norm_g, ada_w, ada_b, ab_w_in, ab_conv_w, ab_ln_g, ab_ln_b, ab_sgu_w,
           ab_sgu_b, ab_w_out, c_w_in, c_pool_w, c_pool_scale, c_w_out, final_g):
    batch = x.shape[0]
    assert x.shape[1] % SEQ_BLOCK == 0 and SEQ_BLOCK % SEQ_TILE == 0
    assert SEQ_TILE % CHUNK == 0
    mod, c_w_mix, *big_w = _prepare_params(
        c, ada_w, ada_b, c_w_in, c_pool_w, c_pool_scale,
        [(ab_w_in, 0, 0, AB_IN), (ab_w_out, 0, 0, D_MODEL)])
    for i in range(DEPTH):
        fg = final_g if i == DEPTH - 1 else None
        nxt = (i + 1) // 2
        if i + 1 == DEPTH:
            cast_jobs = []
        elif i % 2 == 0:
            cast_jobs = [(c_w_in, nxt, 1, E_C), (c_w_out, nxt, 0, D_MODEL)]
        else:
            cast_jobs = [(ab_w_in, nxt, 0, AB_IN), (ab_w_out, nxt, 0, D_MODEL)]
        if i % 2 == 0:
            x, *big_w = _even_layer(i, x, mod, norm_g, big_w[0], ab_conv_w, ab_ln_g,
                                    ab_ln_b, ab_sgu_w, ab_sgu_b, big_w[1], fg, cast_jobs)
        else:
            x, *big_w = _odd_layer(i, x, mod, norm_g, c_w_mix, big_w[0], big_w[1], fg,
                                   cast_jobs)
    return x
```

```python
import functools

import jax
import jax.numpy as jnp
from jax import lax
from jax.experimental import pallas as pl
from jax.experimental.pallas import tpu as pltpu

D_MODEL = 1024
DEPTH = 4
E_INNER = 2 * D_MODEL
HEAD_DIM = 128
E_A = E_INNER // 2
E_B = E_INNER - E_A
H_B = E_B // HEAD_DIM
CONV_WIDTH = 3
CHUNK = 128
AB_IN = 4 * E_A + 3 * E_B
E_C = E_INNER
POOL_WINDOWS = (2, 4, 8, 16)
G_C = E_C // len(POOL_WINDOWS)
EPS = 1e-6

SEQ_BLOCK = 1024
SEQ_TILE = 512
COL_BLOCK = 256
CONV_HALO = 8
POOL_HALO = 16
PREPARE_STEPS = 16
V7X_VMEM_BYTES = 64 * 1024 * 1024
VMEM_LIMIT_BYTES = V7X_VMEM_BYTES - 8 * 1024 * 1024

assert all(w == 2 ** (i + 1) for i, w in enumerate(POOL_WINDOWS))
assert POOL_WINDOWS[-1] <= POOL_HALO and CONV_WIDTH - 1 <= CONV_HALO

_F32 = jnp.float32
_BF16 = jnp.bfloat16


_dot = functools.partial(jnp.dot, preferred_element_type=_F32)


def _silu(v):
    return v * jax.nn.sigmoid(v)


def _rmsnorm(x, g):
    return x * lax.rsqrt(jnp.mean(x * x, axis=-1, keepdims=True) + EPS) * g


def _resident(block_shape, index_map):
    return pl.BlockSpec(block_shape, index_map, pipeline_mode=pl.Buffered(1))


def _layer_rows(mod_ref, g_ref, final_g_ref, layer):
    b = pl.program_id(0)
    mod = [mod_ref[k, pl.ds(b, 1), :] for k in range(3)]
    final_g = None if final_g_ref is None else final_g_ref[...].reshape(1, D_MODEL)
    g = g_ref[layer:layer + 1, :]
    mod.append(g * (1.0 + mod[1]))
    return g, mod, final_g


def _modulated_input(x, g, mod):
    del g
    return x * lax.rsqrt(jnp.mean(x * x, axis=-1, keepdims=True) + EPS) * mod[3] + mod[0]


def _project_out(x_ref, y_scr, w_out_ref, rows, mod, final_g, o_ref, k_block):
    out = None
    for k0 in range(0, y_scr.shape[1], k_block):
        part = _dot(y_scr[rows, k0:k0 + k_block],
                    _weight(w_out_ref, k0, k0 + k_block, slice(None)))
        out = part if out is None else out + part
    res = x_ref[rows] + mod[2] * out
    if final_g is not None:
        res = _rmsnorm(res, final_g)
    o_ref[rows] = res


def _split_refs(rest, final_norm, n_cast):
    rest = list(rest)
    final_g_ref = rest.pop(0) if final_norm else None
    cast_src = [rest.pop(0) for _ in range(n_cast)]
    o_ref = rest.pop(0)
    cast_dst = [rest.pop(0) for _ in range(n_cast)]
    return final_g_ref, cast_src, o_ref, cast_dst, rest


def _pack_weight(w):
    return pltpu.bitcast(w.astype(_BF16), jnp.uint32)


def _weight(w_ref, k0, k1, cols):
    return pltpu.bitcast(w_ref[k0 // 2:k1 // 2, cols], _BF16)


def _cast_next_weights(cast_src, cast_dst):
    for src, dst in zip(cast_src, cast_dst):
        dst[...] = _pack_weight(src[...])


def _shift_rows(ext, k, halo):
    return pltpu.roll(ext, k, axis=0)[halo:]


def _tiles(x_ref):
    n_tiles = x_ref.shape[0] // SEQ_TILE
    return [slice(i * SEQ_TILE, (i + 1) * SEQ_TILE) for i in range(n_tiles)]


def _run_tiles(stages):
    stages[0][0]()
    for ti, (_, body, finish) in enumerate(stages):
        body()
        if ti + 1 < len(stages):
            stages[ti + 1][0]()
        finish()


def _even_kernel(x_ref, mod_ref, g_ref, w_in_ref, conv_ref, lng_ref, lnb_ref,
                 sgw_ref, sgb_ref, w_out_ref, *rest, layer, final_norm, n_cast):
    final_g_ref, cast_src, o_ref, cast_dst, scratch = _split_refs(rest, final_norm, n_cast)
    h_scr, y_scr, tail_scr = scratch
    g, mod, final_g = _layer_rows(mod_ref, g_ref, final_g_ref, layer)
    j = layer // 2
    ts = SEQ_TILE
    n_chunks = ts // CHUNK
    b = pl.program_id(0)
    s = pl.program_id(1)

    @pl.when((b == 0) & (s == 0))
    def _():
        tail_scr[...] = jnp.zeros(tail_scr.shape, _F32)

    row = lax.broadcasted_iota(jnp.int32, (CHUNK, CHUNK), 0)
    col = lax.broadcasted_iota(jnp.int32, (CHUNK, CHUNK), 1)
    causal = row >= col
    base_b = 4 * E_A
    heads_per_block = COL_BLOCK // HEAD_DIM
    n_blocks = E_B // COL_BLOCK
    bias_cols = []
    sgu_w = []
    for head in range(H_B):
        b_row = jnp.broadcast_to(sgb_ref[head:head + 1, :], (CHUNK, CHUNK))
        bias_cols.append(jnp.concatenate([b_row.T] * n_chunks, axis=1))
        sgu_w.append(jnp.where(causal, sgw_ref[head], 0.0).astype(_BF16))

    def mix_tile(ti, rows):
        def proj(seg_start, c0, width=COL_BLOCK):
            cols = slice(seg_start + c0, seg_start + c0 + width)
            return _dot(h_scr[rows], _weight(w_in_ref, 0, D_MODEL, cols))

        def a_block(c0):
            cols = slice(c0, c0 + COL_BLOCK)
            a_h = proj(0 * E_A, c0)
            a_c = proj(2 * E_A, c0)
            a_b = proj(1 * E_A, c0)
            a_z = proj(3 * E_A, c0)
            ch = a_c * a_h
            halo = tail_scr[:, cols]
            if ti == 0:
                halo = jnp.where(s > 0, halo, 0.0)
            tail_scr[:, cols] = ch[ts - CONV_HALO:]
            ext = jnp.concatenate([halo, ch], axis=0)
            w = conv_ref[:, cols]
            conv = (_shift_rows(ext, 2, CONV_HALO) * w[0:1]
                    + _shift_rows(ext, 1, CONV_HALO) * w[1:2] + ch * w[2:3])
            y_scr[rows, cols] = (a_b * conv * _silu(a_z)).astype(_BF16)

        def ln_blocks(c0, n_blk):
            b_v = proj(base_b + 1 * E_B, c0, n_blk * COL_BLOCK)
            rhs = []
            for hh in range(n_blk * heads_per_block):
                glanes = slice(c0 + hh * HEAD_DIM, c0 + (hh + 1) * HEAD_DIM)
                v = b_v[:, hh * HEAD_DIM:(hh + 1) * HEAD_DIM]
                dev = v - jnp.mean(v, axis=-1, keepdims=True)
                var = jnp.mean(dev * dev, axis=-1, keepdims=True)
                vn = (dev * lax.rsqrt(var + EPS) * lng_ref[j:j + 1, glanes]
                      + lnb_ref[j:j + 1, glanes])
                vn = vn.astype(_BF16)
                rhs.append(jnp.concatenate(
                    [vn[n * CHUNK:(n + 1) * CHUNK] for n in range(n_chunks)], axis=1))
            return [rhs[i * heads_per_block:(i + 1) * heads_per_block] for i in range(n_blk)]

        def sgu_block(c0, rhs):
            mixed_heads = []
            for hh in range(heads_per_block):
                head = c0 // HEAD_DIM + hh
                mixed = _dot(sgu_w[head], rhs[hh]) + bias_cols[head]
                mixed_heads.append(jnp.concatenate(
                    [mixed[:, n * CHUNK:(n + 1) * CHUNK] for n in range(n_chunks)], axis=0))
            mixed = jnp.concatenate(mixed_heads, axis=1)
            b_u = proj(base_b + 0 * E_B, c0)
            b_z = proj(base_b + 2 * E_B, c0)
            y_scr[rows, E_A + c0:E_A + c0 + COL_BLOCK] = (
                b_u * mixed * _silu(b_z)).astype(_BF16)

        rhs = []

        def head():
            h_scr[rows] = _modulated_input(x_ref[rows], g, mod).astype(_BF16)
            rhs.extend(ln_blocks(0, 2))

        def body():
            for k in range(n_blocks):
                a_block(k * COL_BLOCK)
                if k + 2 < n_blocks:
                    rhs.extend(ln_blocks((k + 2) * COL_BLOCK, 1))
                sgu_block(k * COL_BLOCK, rhs[k])

        def finish():
            _project_out(x_ref, y_scr, w_out_ref, rows, mod, final_g, o_ref, COL_BLOCK)

        return head, body, finish

    _cast_next_weights(cast_src, cast_dst)
    _run_tiles([mix_tile(ti, rows) for ti, rows in enumerate(_tiles(x_ref))])


def _prepare_kernel(c_ref, ada_w_ref, ada_b_ref, wp_ref, pw_ref, ps_ref, *rest, n_mod_units,
                    n_fold_units):
    n_cast = (len(rest) - 2) // 2
    cast_src, cast_dst = rest[:n_cast], rest[n_cast + 2:]
    mod_ref, wmix_ref = rest[n_cast:n_cast + 2]
    step = pl.program_id(0)

    @pl.when(step < n_mod_units)
    def _():
        c_act = _silu(c_ref[...]).astype(_BF16)
        bias = ada_b_ref[pl.ds(step // 3, 1), :]
        mod_ref[...] = _dot(c_act, ada_w_ref[...].astype(_BF16)) + bias

    @pl.when(step < n_fold_units)
    def _():
        def split(v):
            hi = v.astype(_BF16)
            return hi, (v - hi.astype(_F32)).astype(_BF16)

        a_hi, a_lo = split(wp_ref[...])
        b_hi, b_lo = split(pw_ref[...])
        folded = _dot(a_hi, b_hi) + (_dot(a_hi, b_lo) + _dot(a_lo, b_hi))
        scale = ps_ref[pl.ds(step // len(POOL_WINDOWS), 1), :]
        wmix_ref[...] = _pack_weight(folded * scale)

    _cast_next_weights(cast_src, cast_dst)


def _prepare_params(c, ada_w, ada_b, c_w_in, c_pool_w, c_pool_scale, cast_jobs):
    batch = c.shape[0]
    n_odd = c_w_in.shape[0]
    n_groups = len(POOL_WINDOWS)
    n_mod_units, n_fold_units = DEPTH * 3, n_odd * n_groups
    n_steps = PREPARE_STEPS
    assert n_steps >= max(n_mod_units, n_fold_units)

    def mod_unit(i):
        u = jnp.minimum(i, n_mod_units - 1)
        return u // 3, u % 3

    def fold_unit(i):
        u = jnp.minimum(i, n_fold_units - 1)
        return u // n_groups, u % n_groups

    c_in, c_args, c_out, c_shape = _cast_job_specs(cast_jobs, n_steps, lambda i: i)
    return pl.pallas_call(
        functools.partial(_prepare_kernel, n_mod_units=n_mod_units,
                          n_fold_units=n_fold_units),
        grid=(n_steps,),
        in_specs=[
            pl.BlockSpec((batch, D_MODEL), lambda i: (0, 0)),
            pl.BlockSpec((None, D_MODEL, D_MODEL),
                         lambda i: (mod_unit(i)[0], 0, mod_unit(i)[1])),
            pl.BlockSpec((DEPTH, D_MODEL), lambda i: (0, mod_unit(i)[1])),
            pl.BlockSpec((None, D_MODEL, G_C),
                         lambda i: (fold_unit(i)[0], 0, fold_unit(i)[1])),
            pl.BlockSpec((None, None, G_C, G_C), lambda i: (*fold_unit(i), 0, 0)),
            pl.BlockSpec((n_odd, G_C), lambda i: (0, fold_unit(i)[1])),
        ] + c_in,
        out_specs=[
            pl.BlockSpec((None, None, batch, D_MODEL), lambda i: (*mod_unit(i), 0, 0)),
            pl.BlockSpec((None, D_MODEL // 2, G_C),
                         lambda i: (fold_unit(i)[0], 0, fold_unit(i)[1])),
        ] + c_out,
        out_shape=[
            jax.ShapeDtypeStruct((DEPTH, 3, batch, D_MODEL), _F32),
            jax.ShapeDtypeStruct((n_odd, D_MODEL // 2, E_C), jnp.uint32),
        ] + c_shape,
        compiler_params=pltpu.CompilerParams(
            dimension_semantics=("arbitrary",),
            vmem_limit_bytes=VMEM_LIMIT_BYTES),
        name="prepare_params",
    )(c, ada_w, ada_b, c_w_in, c_pool_w, c_pool_scale, *c_args)


def _odd_kernel(x_ref, mod_ref, g_ref, w_mix_ref, w_z_ref, w_out_ref, *rest, layer,
                final_norm, n_cast):
    final_g_ref, cast_src, o_ref, cast_dst, scratch = _split_refs(rest, final_norm, n_cast)
    h_scr, y_scr, tail_scr, lh_scr = scratch
    g, mod, final_g = _layer_rows(mod_ref, g_ref, final_g_ref, layer)
    ts = SEQ_TILE
    b = pl.program_id(0)
    s = pl.program_id(1)
    n_groups = len(POOL_WINDOWS)
    tiles = _tiles(x_ref)

    @pl.when((b == 0) & (s == 0))
    def _():
        tail_scr[...] = jnp.zeros(tail_scr.shape, _F32)

    def window_sums(level, prev):
        sums, new_prev, k = [], [], 1
        for li, win in enumerate(POOL_WINDOWS):
            new_prev.append(level)
            before = level if prev is None else prev[li]
            ext = jnp.concatenate([before[POOL_HALO - 8:], level], axis=0)
            shifted = ext[8 - k:8 - k + POOL_HALO] if k == 8 else _shift_rows(ext, k, 8)
            level = level + shifted
            sums.append(level)
            k *= 2
        return sums, new_prev

    def prepare(ti, rows):
        halo = tail_scr[...]
        if ti == 0:
            halo = jnp.where(s > 0, halo, 0.0)
        _, prev = window_sums(halo, None)
        first_pos = (s * len(tiles) + ti) * ts
        for r0 in range(0, ts, POOL_HALO):
            grp = slice(rows.start + r0, rows.start + r0 + POOL_HALO)
            h = _modulated_input(x_ref[grp], g, mod)
            h_scr[grp] = h.astype(_BF16)
            sums, prev = window_sums(h, prev)
            pos = (first_pos + r0
                   + lax.broadcasted_iota(jnp.int32, (POOL_HALO, HEAD_DIM), 0))
            for gi, win in enumerate(POOL_WINDOWS):
                cnt = jnp.minimum(pos + 1, win).astype(_F32)
                inv_cnt = jnp.concatenate([1.0 / cnt] * (D_MODEL // HEAD_DIM), axis=1)
                lh_scr[gi, grp] = (sums[gi] * inv_cnt - h).astype(_BF16)
        tail_scr[...] = h

    def mix_tile(ti, rows):
        def gate_group(gi):
            gcols = slice(gi * G_C, (gi + 1) * G_C)
            return _silu(_dot(h_scr[rows], _weight(w_z_ref, 0, D_MODEL, gcols)))

        def mix_group(gi, gate):
            gcols = slice(gi * G_C, (gi + 1) * G_C)
            mixed = _dot(lh_scr[gi, rows], _weight(w_mix_ref, 0, D_MODEL, gcols))
            y_scr[rows, gcols] = (mixed * gate).astype(_BF16)

        lead = min(2, n_groups)
        gates = []

        def head():
            prepare(ti, rows)
            gates.extend(gate_group(gi) for gi in range(lead))
            mix_group(0, gates[0])

        def body():
            for gi in range(1, n_groups):
                if gi - 1 + lead < n_groups:
                    gates.append(gate_group(gi - 1 + lead))
                mix_group(gi, gates[gi])

        def finish():
            _project_out(x_ref, y_scr, w_out_ref, rows, mod, final_g, o_ref, G_C)

        return head, body, finish

    _cast_next_weights(cast_src, cast_dst)
    _run_tiles([mix_tile(ti, rows) for ti, rows in enumerate(tiles)])


def _cast_job_specs(cast_jobs, n_steps, step_of):
    in_specs, args, out_specs, out_shape = [], [], [], []
    for src, j, col_block, n_cols in cast_jobs:
        n_rows = src.shape[1]
        rows = n_rows // n_steps
        assert rows * n_steps == n_rows and rows % 16 == 0
        in_specs.append(pl.BlockSpec(
            (None, rows, n_cols),
            lambda *g, j=j, cb=col_block: (j, step_of(*g), cb)))
        args.append(src)
        out_specs.append(pl.BlockSpec((rows // 2, n_cols), lambda *g: (step_of(*g), 0)))
        out_shape.append(jax.ShapeDtypeStruct((n_rows // 2, n_cols), jnp.uint32))
    return in_specs, args, out_specs, out_shape


def _layer_call(kernel_fn, name, layer, x, mod, norm_g, weights, weight_specs,
                scratch_shapes, final_g, cast_jobs):
    batch, seq, _ = x.shape
    n_seq = seq // SEQ_BLOCK
    n_steps = batch * n_seq
    x_spec = pl.BlockSpec((None, SEQ_BLOCK, D_MODEL), lambda b, s: (b, s, 0))
    in_specs = [
        x_spec,
        _resident((None, 3, batch, D_MODEL), lambda b, s: (layer, 0, 0, 0)),
        _resident((DEPTH, D_MODEL), lambda b, s: (0, 0)),
    ] + weight_specs
    args = [x, mod, norm_g] + weights
    if final_g is not None:
        in_specs.append(_resident((D_MODEL,), lambda b, s: (0,)))
        args.append(final_g)
    c_in, c_args, c_out, c_shape = _cast_job_specs(
        cast_jobs, n_steps, lambda b, s: b * n_seq + s)
    in_specs += c_in
    args += c_args
    out_specs = [x_spec] + c_out
    out_shape = [jax.ShapeDtypeStruct(x.shape, x.dtype)] + c_shape
    return pl.pallas_call(
        functools.partial(kernel_fn, layer=layer, final_norm=final_g is not None,
                          n_cast=len(cast_jobs)),
        grid=(batch, n_seq),
        in_specs=in_specs,
        out_specs=out_specs,
        out_shape=out_shape,
        scratch_shapes=[
            pltpu.VMEM((SEQ_BLOCK, D_MODEL), _BF16),
            pltpu.VMEM((SEQ_BLOCK, E_INNER), _BF16),
        ] + scratch_shapes,
        compiler_params=pltpu.CompilerParams(
            dimension_semantics=("arbitrary", "arbitrary"),
            vmem_limit_bytes=VMEM_LIMIT_BYTES),
        name=name,
    )(*args)


def _even_layer(layer, x, mod, norm_g, w_in, conv_w, ln_g, ln_b, sgu_w, sgu_b, w_out,
                final_g, cast_jobs):
    j = layer // 2
    n_even = conv_w.shape[0]
    const2 = lambda b, s: (0, 0)
    sel2 = lambda b, s: (j, 0, 0)
    sel3 = lambda b, s: (j, 0, 0, 0)
    weights = [w_in, conv_w, ln_g, ln_b, sgu_w, sgu_b, w_out]
    specs = [
        _resident((D_MODEL // 2, AB_IN), const2),
        _resident((None, CONV_WIDTH, E_A), sel2),
        _resident((n_even, E_B), const2),
        _resident((n_even, E_B), const2),
        _resident((None, H_B, CHUNK, CHUNK), sel3),
        _resident((None, H_B, CHUNK), sel2),
        _resident((E_INNER // 2, D_MODEL), const2),
    ]
    scratch = [pltpu.VMEM((CONV_HALO, E_A), _F32)]
    return _layer_call(_even_kernel, "even_layer", layer, x, mod, norm_g, weights, specs,
                       scratch, final_g, cast_jobs)


def _odd_layer(layer, x, mod, norm_g, w_mix, w_z, w_out, final_g, cast_jobs):
    j = layer // 2
    const2 = lambda b, s: (0, 0)
    weights = [w_mix, w_z, w_out]
    specs = [
        _resident((None, D_MODEL // 2, E_C), lambda b, s: (j, 0, 0)),
        _resident((D_MODEL // 2, E_C), const2),
        _resident((E_C // 2, D_MODEL), const2),
    ]
    scratch = [
        pltpu.VMEM((POOL_HALO, D_MODEL), _F32),
        pltpu.VMEM((len(POOL_WINDOWS), SEQ_BLOCK, D_MODEL), _BF16),
    ]
    return _layer_call(_odd_kernel, "odd_layer", layer, x, mod, norm_g, weights, specs,
                       scratch, final_g, cast_jobs)


def kernel(x, c, norm_g, ada_w, ada_b, ab_w_in, ab_conv_w, ab_ln_g, ab_ln_b, ab_sgu_w,
           ab_sgu_b, ab_w_out, c_w_in, c_pool_w, c_pool_scale, c_w_out, final_g):
    batch = x.shape[0]
    assert x.shape[1] % SEQ_BLOCK == 0 and SEQ_BLOCK % SEQ_TILE == 0
    assert SEQ_TILE % CHUNK == 0
    mod, c_w_mix, *big_w = _prepare_params(
        c, ada_w, ada_b, c_w_in, c_pool_w, c_pool_scale,
        [(ab_w_in, 0, 0, AB_IN), (ab_w_out, 0, 0, D_MODEL)])
    for i in range(DEPTH):
        fg = final_g if i == DEPTH - 1 else None
        nxt = (i + 1) // 2
        if i + 1 == DEPTH:
            cast_jobs = []
        elif i % 2 == 0:
            cast_jobs = [(c_w_in, nxt, 1, E_C), (c_w_out, nxt, 0, D_MODEL)]
        else:
            cast_jobs = [(ab_w_in, nxt, 0, AB_IN), (ab_w_out, nxt, 0, D_MODEL)]
        if i % 2 == 0:
            x, *big_w = _even_layer(i, x, mod, norm_g, big_w[0], ab_conv_w, ab_ln_g,
                                    ab_ln_b, ab_sgu_w, ab_sgu_b, big_w[1], fg, cast_jobs)
        else:
            x, *big_w = _odd_layer(i, x, mod, norm_g, c_w_mix, big_w[0], big_w[1], fg,
                                   cast_jobs)
    return x
```
